```python
import math, functools
import jax, jax.numpy as jnp
from jax import lax
import numpy as np

D_MODEL = 2048
BATCH = 8
SEQ = 2048
DEPTH = 2
DEC_BATCH = 128
DEC_SEQ = 8
PAST_LEN = 2048
PAGE_SIZE = 128

N_EVEN = (DEPTH + 1) // 2
N_ODD = DEPTH // 2

H_A = 16
DH_A = 64
W_A = H_A * DH_A
LORA_W = 64
LORA_A = 64
LORA_G = 128
P_A = 3 * W_A + LORA_W + LORA_A + LORA_G
GN_EPS_A = 64e-5

H_B = 16
DH_B = 64
W_B = H_B * DH_B
P_B = 3 * W_B

H_C = 16
DH_C = 64
W_C = 2 * H_C * DH_C
P_C = 6 * H_C * DH_C
SUBLN_EPS = 1e-5

NUM_BUCKETS = 32
MAX_DISTANCE = 128

D_FF = 5632
N_EXPERTS = 8
TOP_K = 2
D_FF_EXPERT = 7168

PLE_DIM = 256

ALPHA = (2 * DEPTH) ** 0.25
DEEPNORM_BETA = (8 * DEPTH) ** -0.25

LN_EPS = 1e-5
Q_BLOCK = 128
NEG_INF = -1e30
F32 = jnp.float32

kernel_name = 'rwkv7_stickbreak_diffattn_hybrid_step'


def layer_norm(x, g, b):
    xf = x.astype(F32)
    mu = jnp.mean(xf, -1, keepdims=True)
    var = jnp.mean(jnp.square(xf - mu), -1, keepdims=True)
    return ((xf - mu) * lax.rsqrt(var + LN_EPS) * g.astype(F32) + b.astype(F32)).astype(x.dtype)


def swiglu(h, wg, wu, wd):
    return (jax.nn.silu(h @ wg) * (h @ wu)) @ wd


def moe_swiglu(h, router, wg, wu, wd):
    logits = (h @ router).astype(F32)
    top_v, top_i = lax.top_k(logits, TOP_K)
    gates = jax.nn.softmax(top_v, axis=-1)
    combine = jnp.sum(jax.nn.one_hot(top_i, N_EXPERTS, dtype=F32) * gates[..., None], axis=-2).astype(h.dtype)
    out = jnp.zeros_like(h)
    for e in range(N_EXPERTS):
        out = out + combine[..., e:e + 1] * swiglu(h, wg[e], wu[e], wd[e])
    return out


def per_layer_embed(h, p_i, w_gate, w_proj):
    return jax.nn.sigmoid(h @ w_gate) * (p_i @ w_proj)


def gather_pages(pool, page_table):
    g = pool[page_table]
    return g.reshape((g.shape[0], g.shape[1] * g.shape[2]) + pool.shape[2:])


def t5_bias(q_pos, k_pos, rel_bias):
    n = jnp.maximum(q_pos[:, None] - k_pos[None, :], 0)
    max_exact = NUM_BUCKETS // 2
    nf = jnp.maximum(n, 1).astype(F32)
    far = max_exact + (jnp.log(nf / max_exact) / math.log(MAX_DISTANCE / max_exact)
                       * (NUM_BUCKETS - max_exact)).astype(jnp.int32)
    bucket = jnp.where(n < max_exact, n, jnp.minimum(far, NUM_BUCKETS - 1))
    return jnp.transpose(rel_bias.astype(F32)[bucket], (2, 0, 1))


def stick_breaking_block(q, k, v, q_pos, k_pos):
    z = jnp.einsum('bqhd,bkhd->bhqk', q.astype(F32), k.astype(F32)) * (DH_B ** -0.5)
    mask = k_pos[None, :] < q_pos[:, None]
    log_keep = jnp.where(mask, jax.nn.log_sigmoid(-z), 0.0)
    between = lax.cumsum(log_keep, axis=3, reverse=True) - log_keep
    weight = jnp.where(mask, jnp.exp(jax.nn.log_sigmoid(z) + between), 0.0)
    return jnp.einsum('bhqk,bkhd->bqhd', weight, v.astype(F32))


def diff_attn_block(q, k, v, q_pos, k_pos, lam, rel_bias):
    bsz, nq = q.shape[0], q.shape[1]
    nk = k.shape[1]
    s = jnp.einsum('bqhd,bkhd->bhqk', q.astype(F32), k.astype(F32)) * (DH_C ** -0.5)
    s = s.reshape(bsz, H_C, 2, nq, nk) + t5_bias(q_pos, k_pos, rel_bias)[None, :, None]
    mask = k_pos[None, :] <= q_pos[:, None]
    probs = jax.nn.softmax(jnp.where(mask, s, NEG_INF), axis=-1)
    attn = probs[:, :, 0] - lam * probs[:, :, 1]
    return jnp.einsum('bhqk,bkhe->bqhe', attn, v.astype(F32))


def blocked_causal(block_fn, q, k_all, v_all, past_len):
    t_q = q.shape[1]
    qb = min(Q_BLOCK, t_q)
    outs = []
    for s in range(0, t_q, qb):
        e = min(s + qb, t_q)
        n_k = past_len + e
        q_pos = past_len + jnp.arange(s, e, dtype=jnp.int32)
        k_pos = jnp.arange(n_k, dtype=jnp.int32)
        outs.append(block_fn(q[:, s:e], k_all[:, :n_k], v_all[:, :n_k], q_pos, k_pos))
    return jnp.concatenate(outs, axis=1)


def rwkv7_step(S, inp):
    r_t, w_t, k_t, v_t, a_t, b_t = inp
    sa = jnp.einsum('bhij,bhj->bhi', S, a_t)
    S = S * w_t[:, :, None, :] + sa[..., None] * b_t[:, :, None, :] + v_t[..., None] * k_t[:, :, None, :]
    y = jnp.einsum('bhij,bhj->bhi', S, r_t)
    return S, y


def rwkv7_mix(u, shift_prev, s_prev, W, j):
    bsz, t_len, _ = u.shape
    c = lambda name: W[name][j].astype(F32)
    uf = u.astype(F32)
    prev = jnp.concatenate([shift_prev.astype(F32)[:, None, :], uf[:, :-1]], axis=1)
    m = uf + (prev - uf) * c('rwkv_mu')
    o0 = 3 * W_A
    o1 = o0 + LORA_W
    o2 = o1 + LORA_A
    r, k, v = m[..., :W_A], m[..., W_A:2 * W_A], m[..., 2 * W_A:o0]
    xw, xa, xg = m[..., o0:o1], m[..., o1:o2], m[..., o2:]
    w_log = -jax.nn.softplus(-(c('rwkv_w0') + jnp.tanh(xw) @ c('rwkv_w2'))) - 0.5
    decay = jnp.exp(-jnp.exp(w_log))
    a = jax.nn.sigmoid(c('rwkv_a0') + xa @ c('rwkv_a2'))
    g = jax.nn.sigmoid(xg) @ c('rwkv_g2')
    heads = lambda t: t.reshape(bsz, t_len, H_A, DH_A)
    kk = heads(k * c('rwkv_k_k'))
    kk = kk * lax.rsqrt(jnp.maximum(jnp.sum(jnp.square(kk), -1, keepdims=True), 1e-24))
    k = k * (1.0 + (a - 1.0) * c('rwkv_k_a'))
    r_h, k_h, v_h, a_h = heads(r), heads(k), heads(v), heads(a)
    seq = tuple(jnp.moveaxis(t, 1, 0) for t in (r_h, heads(decay), k_h, v_h, -kk, kk * a_h))
    s_last, y = lax.scan(rwkv7_step, s_prev.astype(F32), seq)
    y = jnp.moveaxis(y, 0, 1)
    mu = jnp.mean(y, -1, keepdims=True)
    var = jnp.mean(jnp.square(y - mu), -1, keepdims=True)
    y = ((y - mu) * lax.rsqrt(var + GN_EPS_A)).reshape(bsz, t_len, W_A) * c('rwkv_gn_g') + c('rwkv_gn_b')
    bonus = jnp.sum(r_h * k_h * c('rwkv_r_k'), -1, keepdims=True) * v_h
    out = (y + bonus.reshape(bsz, t_len, W_A)) * g
    return out, s_last, u[:, -1]


def even_layer(x, p_i, past_k, past_v, s_prev, shift_prev, i, W):
    j = i // 2
    bsz, t_len, _ = x.shape
    past_len = past_k.shape[1]
    proj = x @ W['w_in_even'][j]
    u_a = proj[..., :P_A]
    qkv_b = proj[..., P_A:].reshape(bsz, t_len, 3, H_B, DH_B)
    q_b, k_b, v_b = qkv_b[:, :, 0], qkv_b[:, :, 1], qkv_b[:, :, 2]
    o_a, s_new, shift_new = rwkv7_mix(u_a, shift_prev, s_prev, W, j)
    k_all = jnp.concatenate([past_k.astype(k_b.dtype), k_b], axis=1)
    v_all = jnp.concatenate([past_v.astype(v_b.dtype), v_b], axis=1)
    o_b = blocked_causal(stick_breaking_block, q_b, k_all, v_all, past_len)
    mix = jnp.concatenate([o_a.astype(x.dtype), o_b.reshape(bsz, t_len, W_B).astype(x.dtype)], axis=-1) @ W['w_out_even'][j]
    h = layer_norm(ALPHA * x + mix, W['ln1_g'][i], W['ln1_b'][i])
    f = swiglu(h, W['ffn_gate'][j], W['ffn_up'][j], W['ffn_down'][j])
    e = per_layer_embed(h, p_i, W['ple_gate'][i], W['ple_proj'][i])
    y = layer_norm(ALPHA * h + f + e, W['ln2_g'][i], W['ln2_b'][i])
    return y, k_b, v_b, s_new.astype(x.dtype), shift_new


def odd_layer(x, p_i, past_k, past_v, i, W):
    j = i // 2
    bsz, t_len, _ = x.shape
    past_len = past_k.shape[1]
    hd = H_C * DH_C
    proj = x @ W['w_in_odd'][j]
    q = proj[..., :2 * hd].reshape(bsz, t_len, 2 * H_C, DH_C)
    k = proj[..., 2 * hd:4 * hd].reshape(bsz, t_len, 2 * H_C, DH_C)
    v = proj[..., 4 * hd:].reshape(bsz, t_len, H_C, 2 * DH_C)
    lam_init = 0.8 - 0.6 * math.exp(-0.3 * i)
    c = lambda name: W[name][j].astype(F32)
    lam = (jnp.exp(jnp.sum(c('diff_lq1') * c('diff_lk1'))) - jnp.exp(jnp.sum(c('diff_lq2') * c('diff_lk2')))
           + lam_init)
    k_all = jnp.concatenate([past_k.astype(k.dtype), k], axis=1)
    v_all = jnp.concatenate([past_v.astype(v.dtype), v], axis=1)
    block = functools.partial(diff_attn_block, lam=lam, rel_bias=W['rel_bias'])
    o = blocked_causal(block, q, k_all, v_all, past_len)
    o = o * lax.rsqrt(jnp.mean(jnp.square(o), -1, keepdims=True) + SUBLN_EPS) * c('diff_subln_g') * (1.0 - lam_init)
    mix = o.reshape(bsz, t_len, W_C).astype(x.dtype) @ W['w_out_odd'][j]
    h = layer_norm(ALPHA * x + mix, W['ln1_g'][i], W['ln1_b'][i])
    f = moe_swiglu(h, W['moe_router'][j], W['moe_gate'][j], W['moe_up'][j], W['moe_down'][j])
    e = per_layer_embed(h, p_i, W['ple_gate'][i], W['ple_proj'][i])
    y = layer_norm(ALPHA * h + f + e, W['ln2_g'][i], W['ln2_b'][i])
    return y, k, v


def trunk(x, p, sb_k_past, sb_v_past, dk_past, dv_past, s_past, shift_past, W):
    sb_k, sb_v, dk, dv, st, sh = [], [], [], [], [], []
    for i in range(DEPTH):
        j = i // 2
        if i % 2 == 0:
            x, kb, vb, s_new, sh_new = even_layer(x, p[i], sb_k_past[j], sb_v_past[j], s_past[j], shift_past[j], i, W)
            sb_k.append(kb)
            sb_v.append(vb)
            st.append(s_new)
            sh.append(sh_new)
        else:
            x, kc, vc = odd_layer(x, p[i], dk_past[j], dv_past[j], i, W)
            dk.append(kc)
            dv.append(vc)
    return x, jnp.stack(sb_k), jnp.stack(sb_v), jnp.stack(dk), jnp.stack(dv), jnp.stack(st), jnp.stack(sh)


def setup_inputs(seed: int = 0) -> dict:
    key = jax.random.key(seed)
    ks = iter(jax.random.split(key, 64))
    nrm = lambda shape, scale: jax.random.normal(next(ks), shape, jnp.float32) * scale
    uni = lambda shape, lo, hi: jax.random.uniform(next(ks), shape, jnp.float32, lo, hi)
    n_pages = PAST_LEN // PAGE_SIZE
    n_pool = (DEC_BATCH * n_pages * 5) // 4
    perm = jax.random.permutation(next(ks), n_pool)
    page_table = perm[:DEC_BATCH * n_pages].reshape(DEC_BATCH, n_pages).astype(jnp.int32)
    d = D_MODEL
    inp = {}
    inp['x_prompt'] = nrm((BATCH, SEQ, d), 1.0)
    inp['x_sample'] = nrm((DEC_BATCH, DEC_SEQ, d), 1.0)
    inp['cache_sb_k'] = nrm((N_EVEN, n_pool, PAGE_SIZE, H_B, DH_B), 1.0)
    inp['cache_sb_v'] = nrm((N_EVEN, n_pool, PAGE_SIZE, H_B, DH_B), 1.0)
    inp['cache_diff_k'] = nrm((N_ODD, n_pool, PAGE_SIZE, 2 * H_C, DH_C), 1.0)
    inp['cache_diff_v'] = nrm((N_ODD, n_pool, PAGE_SIZE, H_C, 2 * DH_C), 1.0)
    inp['state_rwkv'] = nrm((N_EVEN, DEC_BATCH, H_A, DH_A, DH_A), 0.5)
    inp['state_rwkv_shift'] = nrm((N_EVEN, DEC_BATCH, P_A), 1.0)
    inp['page_table'] = page_table
    inp['p_prompt'] = nrm((DEPTH, BATCH, SEQ, PLE_DIM), 1.0)
    inp['p_sample'] = nrm((DEPTH, DEC_BATCH, DEC_SEQ, PLE_DIM), 1.0)
    inp['w_in_even'] = nrm((N_EVEN, d, P_A + P_B), d ** -0.5)
    inp['w_out_even'] = nrm((N_EVEN, W_A + W_B, d), (W_A + W_B) ** -0.5 * DEEPNORM_BETA)
    inp['rwkv_mu'] = uni((N_EVEN, P_A), 0.0, 1.0)
    inp['rwkv_w0'] = uni((N_EVEN, W_A), -4.0, 1.0)
    inp['rwkv_w2'] = nrm((N_EVEN, LORA_W, W_A), LORA_W ** -0.5)
    inp['rwkv_a0'] = nrm((N_EVEN, W_A), 0.5)
    inp['rwkv_a2'] = nrm((N_EVEN, LORA_A, W_A), LORA_A ** -0.5)
    inp['rwkv_g2'] = nrm((N_EVEN, LORA_G, W_A), LORA_G ** -0.5)
    inp['rwkv_k_k'] = 0.85 + nrm((N_EVEN, W_A), 0.1)
    inp['rwkv_k_a'] = 1.0 + nrm((N_EVEN, W_A), 0.1)
    inp['rwkv_r_k'] = nrm((N_EVEN, H_A, DH_A), 0.1)
    inp['rwkv_gn_g'] = 1.0 + nrm((N_EVEN, W_A), 0.1)
    inp['rwkv_gn_b'] = nrm((N_EVEN, W_A), 0.02)
    inp['w_in_odd'] = nrm((N_ODD, d, P_C), d ** -0.5)
    inp['w_out_odd'] = nrm((N_ODD, W_C, d), W_C ** -0.5 * DEEPNORM_BETA)
    inp['diff_lq1'] = nrm((N_ODD, DH_C), 0.1)
    inp['diff_lk1'] = nrm((N_ODD, DH_C), 0.1)
    inp['diff_lq2'] = nrm((N_ODD, DH_C), 0.1)
    inp['diff_lk2'] = nrm((N_ODD, DH_C), 0.1)
    inp['diff_subln_g'] = 1.0 + nrm((N_ODD, 2 * DH_C), 0.1)
    inp['rel_bias'] = nrm((NUM_BUCKETS, H_C), 0.5)
    inp['ffn_gate'] = nrm((N_EVEN, d, D_FF), d ** -0.5)
    inp['ffn_up'] = nrm((N_EVEN, d, D_FF), d ** -0.5)
    inp['ffn_down'] = nrm((N_EVEN, D_FF, d), D_FF ** -0.5 * DEEPNORM_BETA)
    inp['moe_router'] = nrm((N_ODD, d, N_EXPERTS), d ** -0.5)
    inp['moe_gate'] = nrm((N_ODD, N_EXPERTS, d, D_FF_EXPERT), d ** -0.5)
    inp['moe_up'] = nrm((N_ODD, N_EXPERTS, d, D_FF_EXPERT), d ** -0.5)
    inp['moe_down'] = nrm((N_ODD, N_EXPERTS, D_FF_EXPERT, d), D_FF_EXPERT ** -0.5 * DEEPNORM_BETA)
    inp['ple_gate'] = nrm((DEPTH, d, d), d ** -0.5)
    inp['ple_proj'] = nrm((DEPTH, PLE_DIM, d), PLE_DIM ** -0.5 * DEEPNORM_BETA)
    inp['ln1_g'] = 1.0 + nrm((DEPTH, d), 0.05)
    inp['ln1_b'] = nrm((DEPTH, d), 0.02)
    inp['ln2_g'] = 1.0 + nrm((DEPTH, d), 0.05)
    inp['ln2_b'] = nrm((DEPTH, d), 0.02)
    return inp


def reference(x_prompt, x_sample, cache_sb_k, cache_sb_v, cache_diff_k, cache_diff_v, state_rwkv, state_rwkv_shift,
              page_table, p_prompt, p_sample,
              w_in_even, w_out_even, rwkv_mu, rwkv_w0, rwkv_w2, rwkv_a0, rwkv_a2, rwkv_g2, rwkv_k_k, rwkv_k_a,
              rwkv_r_k, rwkv_gn_g, rwkv_gn_b,
              w_in_odd, w_out_odd, diff_lq1, diff_lk1, diff_lq2, diff_lk2, diff_subln_g, rel_bias,
              ffn_gate, ffn_up, ffn_down, moe_router, moe_gate, moe_up, moe_down,
              ple_gate, ple_proj, ln1_g, ln1_b, ln2_g, ln2_b):
    W = dict(w_in_even=w_in_even, w_out_even=w_out_even, rwkv_mu=rwkv_mu, rwkv_w0=rwkv_w0, rwkv_w2=rwkv_w2,
             rwkv_a0=rwkv_a0, rwkv_a2=rwkv_a2, rwkv_g2=rwkv_g2, rwkv_k_k=rwkv_k_k, rwkv_k_a=rwkv_k_a,
             rwkv_r_k=rwkv_r_k, rwkv_gn_g=rwkv_gn_g, rwkv_gn_b=rwkv_gn_b,
             w_in_odd=w_in_odd, w_out_odd=w_out_odd, diff_lq1=diff_lq1, diff_lk1=diff_lk1,
             diff_lq2=diff_lq2, diff_lk2=diff_lk2, diff_subln_g=diff_subln_g, rel_bias=rel_bias,
             ffn_gate=ffn_gate, ffn_up=ffn_up, ffn_down=ffn_down, moe_router=moe_router,
             moe_gate=moe_gate, moe_up=moe_up, moe_down=moe_down,
             ple_gate=ple_gate, ple_proj=ple_proj, ln1_g=ln1_g, ln1_b=ln1_b, ln2_g=ln2_g, ln2_b=ln2_b)
    bp = x_prompt.shape[0]
    dt = x_prompt.dtype
    empty_sb = jnp.zeros((bp, 0, H_B, DH_B), dt)
    empty_dk = jnp.zeros((bp, 0, 2 * H_C, DH_C), dt)
    empty_dv = jnp.zeros((bp, 0, H_C, 2 * DH_C), dt)
    zero_s = jnp.zeros((bp, H_A, DH_A, DH_A), F32)
    zero_shift = jnp.zeros((bp, P_A), dt)
    y_prompt, sbk_p, sbv_p, dk_p, dv_p, st_p, sh_p = trunk(
        x_prompt, p_prompt, [empty_sb] * N_EVEN, [empty_sb] * N_EVEN, [empty_dk] * N_ODD, [empty_dv] * N_ODD,
        [zero_s] * N_EVEN, [zero_shift] * N_EVEN, W)
    y_sample, sbk_s, sbv_s, dk_s, dv_s, st_s, sh_s = trunk(
        x_sample, p_sample,
        [gather_pages(cache_sb_k[j], page_table) for j in range(N_EVEN)],
        [gather_pages(cache_sb_v[j], page_table) for j in range(N_EVEN)],
        [gather_pages(cache_diff_k[j], page_table) for j in range(N_ODD)],
        [gather_pages(cache_diff_v[j], page_table) for j in range(N_ODD)],
        [state_rwkv[j] for j in range(N_EVEN)],
        [state_rwkv_shift[j] for j in range(N_EVEN)], W)
    return (y_prompt, y_sample, sbk_p, sbv_p, dk_p, dv_p, st_p, sh_p, sbk_s, sbv_s, dk_s, dv_s, st_s, sh_s)
```

```python
import functools
import math

import jax
import jax.numpy as jnp
from jax import lax
from jax.experimental import pallas as pl
from jax.experimental.pallas import tpu as pltpu

F32 = jnp.float32
BF16 = jnp.bfloat16
I32 = jnp.int32

LANES = 128
VMEM_LIMIT = 56 * 1024 * 1024

LN_EPS = 1e-5
SUBLN_EPS = 1e-5
GN_EPS_A = 64e-5
NEG_INF = -1e30
NUM_BUCKETS = 32
MAX_DISTANCE = 128
TOP_K = 2


def _pick(n, cands):
    for c in cands:
        if n % c == 0:
            return c
    raise ValueError(f"no tile in {cands} divides {n}")


def _params(sem):
    return pltpu.CompilerParams(dimension_semantics=sem, vmem_limit_bytes=VMEM_LIMIT)


def _dot(a, b):
    return jnp.dot(a, b, preferred_element_type=F32)


def _dot_nt(a, b):
    return lax.dot_general(a, b, (((1,), (1,)), ((), ())), preferred_element_type=F32)


def _split3(x):
    hi = x.astype(BF16)
    r1 = x - hi.astype(F32)
    mid = r1.astype(BF16)
    lo = (r1 - mid.astype(F32)).astype(BF16)
    return hi, mid, lo


def _dot_exact_rhs01(x, m01):
    hi, mid, lo = _split3(x)
    return _dot(hi, m01) + _dot(mid, m01) + _dot(lo, m01)


def _layer_norm(z, g, b):
    mu = jnp.mean(z, -1, keepdims=True)
    d = z - mu
    var = jnp.mean(d * d, -1, keepdims=True)
    return d * lax.rsqrt(var + LN_EPS) * g + b


def _sigmoid(x):
    return 1.0 / (1.0 + jnp.exp(-x))


def _mm_kernel(x_ref, w_ref, o_ref):
    o_ref[...] = _dot(x_ref[...], w_ref[...]).astype(o_ref.dtype)


def _matmul(x, w, out_dtype=F32):
    m, k = x.shape
    n = w.shape[1]
    tm = _pick(m, (1024, 512, 256, 128))
    tn = _pick(n, (1280, 1024, 768, 512, 256, 128))
    return pl.pallas_call(
        _mm_kernel,
        grid=(m // tm, n // tn),
        in_specs=[pl.BlockSpec((tm, k), lambda i, j: (i, 0)),
                  pl.BlockSpec((k, tn), lambda i, j: (0, j))],
        out_specs=pl.BlockSpec((tm, tn), lambda i, j: (i, j)),
        out_shape=jax.ShapeDtypeStruct((m, n), out_dtype),
        compiler_params=_params(("parallel", "parallel")),
        name="proj_matmul",
    )(x, w)


def _mm_ln_kernel(a_ref, w_ref, x_ref, g_ref, b_ref, h_ref, hb_ref, *, alpha):
    z = alpha * x_ref[...] + _dot(a_ref[...], w_ref[...])
    h = _layer_norm(z, g_ref[...], b_ref[...])
    h_ref[...] = h
    hb_ref[...] = h.astype(BF16)


def _matmul_residual_ln(a, w, x, g, b, alpha):
    m, k = a.shape
    n = w.shape[1]
    tm = _pick(m, (256, 128))
    row = lambda i: (i, 0)
    fixed = lambda i: (0, 0)
    return pl.pallas_call(
        functools.partial(_mm_ln_kernel, alpha=alpha),
        grid=(m // tm,),
        in_specs=[pl.BlockSpec((tm, k), row), pl.BlockSpec((k, n), fixed),
                  pl.BlockSpec((tm, n), row), pl.BlockSpec((1, n), fixed), pl.BlockSpec((1, n), fixed)],
        out_specs=[pl.BlockSpec((tm, n), row), pl.BlockSpec((tm, n), row)],
        out_shape=[jax.ShapeDtypeStruct((m, n), F32), jax.ShapeDtypeStruct((m, n), BF16)],
        compiler_params=_params(("parallel",)),
        name="out_proj_ln",
    )(a, w, x, g.reshape(1, n), b.reshape(1, n))


def _ple_kernel(hb_ref, wg_ref, pb_ref, wp_ref, e_ref):
    gate = _sigmoid(_dot(hb_ref[...], wg_ref[...]))
    e_ref[...] = gate * _dot(pb_ref[...], wp_ref[...])


def _per_layer_embed(hb, wg, pb, wp):
    m, k = hb.shape
    n = wg.shape[1]
    kp = pb.shape[1]
    tm = _pick(m, (1024, 512, 256, 128))
    tn = _pick(n, (1024, 512, 256, 128))
    return pl.pallas_call(
        _ple_kernel,
        grid=(m // tm, n // tn),
        in_specs=[pl.BlockSpec((tm, k), lambda i, j: (i, 0)), pl.BlockSpec((k, tn), lambda i, j: (0, j)),
                  pl.BlockSpec((tm, kp), lambda i, j: (i, 0)), pl.BlockSpec((kp, tn), lambda i, j: (0, j))],
        out_specs=pl.BlockSpec((tm, tn), lambda i, j: (i, j)),
        out_shape=jax.ShapeDtypeStruct((m, n), F32),
        compiler_params=_params(("parallel", "parallel")),
        name="per_layer_embed",
    )(hb, wg, pb, wp)


def _swiglu_tile(hb, wg, wu, wd):
    g = _dot(hb, wg)
    u = _dot(hb, wu)
    a = g * _sigmoid(g) * u
    return _dot(a.astype(BF16), wd)


def _ffn_kernel(hb_ref, wg_ref, wu_ref, wd_ref, f_ref):
    j = pl.program_id(1)
    contrib = _swiglu_tile(hb_ref[...], wg_ref[...], wu_ref[...], wd_ref[...])

    @pl.when(j == 0)
    def _():
        f_ref[...] = contrib

    @pl.when(j > 0)
    def _():
        f_ref[...] += contrib


def _ffn(hb, wg, wu, wd):
    m, d = hb.shape
    f = wg.shape[1]
    tm = _pick(m, (512, 256, 128))
    tf = _pick(f, (512, 256, 128))
    return pl.pallas_call(
        _ffn_kernel,
        grid=(m // tm, f // tf),
        in_specs=[pl.BlockSpec((tm, d), lambda i, j: (i, 0)),
                  pl.BlockSpec((d, tf), lambda i, j: (0, j)),
                  pl.BlockSpec((d, tf), lambda i, j: (0, j)),
                  pl.BlockSpec((tf, d), lambda i, j: (j, 0))],
        out_specs=pl.BlockSpec((tm, d), lambda i, j: (i, 0)),
        out_shape=jax.ShapeDtypeStruct((m, d), F32),
        compiler_params=_params(("parallel", "arbitrary")),
        name="dense_ffn",
    )(hb, wg, wu, wd)


def _add_ln_kernel(h_ref, f_ref, e_ref, g_ref, b_ref, y_ref, yb_ref, *, alpha):
    z = alpha * h_ref[...] + f_ref[...] + e_ref[...]
    y = _layer_norm(z, g_ref[...], b_ref[...])
    y_ref[...] = y
    yb_ref[...] = y.astype(BF16)


def _add_ln(h, f, e, g, b, alpha):
    m, n = h.shape
    tm = _pick(m, (256, 128))
    row = lambda i: (i, 0)
    fixed = lambda i: (0, 0)
    return pl.pallas_call(
        functools.partial(_add_ln_kernel, alpha=alpha),
        grid=(m // tm,),
        in_specs=[pl.BlockSpec((tm, n), row)] * 3 + [pl.BlockSpec((1, n), fixed)] * 2,
        out_specs=[pl.BlockSpec((tm, n), row), pl.BlockSpec((tm, n), row)],
        out_shape=[jax.ShapeDtypeStruct((m, n), F32), jax.ShapeDtypeStruct((m, n), BF16)],
        compiler_params=_params(("parallel",)),
        name="ffn_residual_ln",
    )(h, f, e, g.reshape(1, n), b.reshape(1, n))


def _rwkv_prep_kernel(u_ref, p_ref, mu_ref, w0_ref, a0_ref, kk_ref, ka_ref, rk_ref,
                      w2_ref, a2_ref, g2_ref, seg_ref,
                      r_ref, w_ref, k_ref, v_ref, a_ref, b_ref, g_ref, bonus_ref, *, wa, lora):
    u = u_ref[...]
    m = u + (p_ref[...] - u) * mu_ref[...]
    r = m[:, 0:wa]
    k = m[:, wa:2 * wa]
    v = m[:, 2 * wa:3 * wa]
    xwa = m[:, 3 * wa:3 * wa + lora]
    xg = m[:, 3 * wa + lora:]
    seg = seg_ref[...]
    lw = w0_ref[...] + _dot(jnp.tanh(xwa).astype(BF16), w2_ref[...])
    w_log = -(jnp.maximum(-lw, 0.0) + jnp.log1p(jnp.exp(-jnp.abs(lw)))) - 0.5
    decay = jnp.exp(-jnp.exp(w_log))
    a = _sigmoid(a0_ref[...] + _dot(xwa.astype(BF16), a2_ref[...]))
    g = _dot(_sigmoid(xg).astype(BF16), g2_ref[...])
    kk = k * kk_ref[...]
    ss = _dot_exact_rhs01(kk * kk, seg)
    kk = kk * lax.rsqrt(jnp.maximum(ss, 1e-24))
    k2 = k * (1.0 + (a - 1.0) * ka_ref[...])
    r_ref[...] = r
    w_ref[...] = decay
    k_ref[...] = k2
    v_ref[...] = v
    a_ref[...] = -kk
    b_ref[...] = kk * a
    g_ref[...] = g
    bonus_ref[...] = _dot_exact_rhs01(r * k2 * rk_ref[...], seg) * v


def _rwkv_prep(u, prev, mu, w0, a0, k_k, k_a, r_k, w2p, a2p, g2, seg):
    n, pa = u.shape
    wa = w0.shape[-1]
    lora = w2p.shape[0]
    lg = g2.shape[0]
    tm = _pick(n, (256, 128))
    row = lambda i: (i, 0)
    fixed = lambda i: (0, 0)
    vec = lambda d: pl.BlockSpec((1, d), fixed)
    outs = [jax.ShapeDtypeStruct((n, wa), F32)] * 8
    return pl.pallas_call(
        functools.partial(_rwkv_prep_kernel, wa=wa, lora=lora),
        grid=(n // tm,),
        in_specs=[pl.BlockSpec((tm, pa), row), pl.BlockSpec((tm, pa), row), vec(pa),
                  vec(wa), vec(wa), vec(wa), vec(wa), vec(wa),
                  pl.BlockSpec((lora, wa), fixed), pl.BlockSpec((lora, wa), fixed),
                  pl.BlockSpec((lg, wa), fixed), pl.BlockSpec((wa, wa), fixed)],
        out_specs=[pl.BlockSpec((tm, wa), row)] * 8,
        out_shape=outs,
        compiler_params=_params(("parallel",)),
        name="rwkv_prep",
    )(u, prev, mu.reshape(1, pa), w0.reshape(1, wa), a0.reshape(1, wa), k_k.reshape(1, wa),
      k_a.reshape(1, wa), r_k.reshape(1, wa), w2p, a2p, g2, seg)


def _rwkv_scan_kernel(r_ref, w_ref, k_ref, v_ref, a_ref, b_ref, s0_ref, y_ref, s_ref, yt_ref, *, tc, dh, npairs):
    c = pl.program_id(1)

    @pl.when(c == 0)
    def _():
        s_ref[...] = s0_ref[...]

    wa = v_ref.shape[-1]
    vpad = jnp.concatenate([v_ref[...], jnp.zeros((LANES - tc, wa), F32)], axis=0)
    vt = vpad.T
    lane = lax.broadcasted_iota(I32, (1, LANES), 1)
    first = lane < dh
    yt_ref[...] = jnp.zeros(yt_ref.shape, F32)
    for t in range(tc):
        for hp in range(npairs):
            cs = slice(hp * LANES, (hp + 1) * LANES)
            a_r = a_ref[t:t + 1, cs]
            w_r = w_ref[t:t + 1, cs]
            b_r = b_ref[t:t + 1, cs]
            k_r = k_ref[t:t + 1, cs]
            r_r = r_ref[t:t + 1, cs]
            s = s_ref[0, hp]
            p = s * a_r
            sa0 = jnp.sum(jnp.where(first, p, 0.0), axis=1, keepdims=True)
            sa1 = jnp.sum(jnp.where(first, 0.0, p), axis=1, keepdims=True)
            sa = jnp.where(first, sa0, sa1)
            v0 = vt[hp * LANES:hp * LANES + dh, t:t + 1]
            v1 = vt[hp * LANES + dh:(hp + 1) * LANES, t:t + 1]
            vb = jnp.where(first, v0, v1)
            s = s * w_r + sa * b_r + vb * k_r
            s_ref[0, hp] = s
            q = s * r_r
            yt_ref[hp * LANES:hp * LANES + dh, t:t + 1] = jnp.sum(jnp.where(first, q, 0.0), axis=1, keepdims=True)
            yt_ref[hp * LANES + dh:(hp + 1) * LANES, t:t + 1] = jnp.sum(jnp.where(first, 0.0, q), axis=1, keepdims=True)
    y_ref[...] = yt_ref[...].T[0:tc, :]


def _rwkv_scan(seqs, s0, row0, nb, t_len, tc, dh):
    wa = seqs[0].shape[1]
    npairs = wa // LANES
    nchunks = t_len // tc
    blk0 = row0 // tc
    seq_spec = pl.BlockSpec((tc, wa), lambda b, c: (blk0 + b * nchunks + c, 0))
    st_spec = pl.BlockSpec((1, npairs, dh, LANES), lambda b, c: (b, 0, 0, 0))
    return pl.pallas_call(
        functools.partial(_rwkv_scan_kernel, tc=tc, dh=dh, npairs=npairs),
        grid=(nb, nchunks),
        in_specs=[seq_spec] * 6 + [st_spec],
        out_specs=[pl.BlockSpec((tc, wa), lambda b, c: (b * nchunks + c, 0)), st_spec],
        out_shape=[jax.ShapeDtypeStruct((nb * t_len, wa), F32),
                   jax.ShapeDtypeStruct((nb, npairs, dh, LANES), F32)],
        scratch_shapes=[pltpu.VMEM((wa, LANES), F32)],
        compiler_params=_params(("parallel", "arbitrary")),
        name="rwkv_scan",
    )(*seqs, s0)


def _rwkv_post_kernel(y_ref, bonus_ref, g_ref, gng_ref, gnb_ref, seg_ref, o_ref, *, dh):
    y = y_ref[...]
    seg = seg_ref[...]
    mu = _dot_exact_rhs01(y, seg) * (1.0 / dh)
    d = y - mu
    var = _dot_exact_rhs01(d * d, seg) * (1.0 / dh)
    yn = d * lax.rsqrt(var + GN_EPS_A) * gng_ref[...] + gnb_ref[...]
    o_ref[...] = ((yn + bonus_ref[...]) * g_ref[...]).astype(o_ref.dtype)


def _rwkv_post(y, bonus, g, gn_g, gn_b, seg, dh):
    n, wa = y.shape
    tm = _pick(n, (256, 128))
    row = lambda i: (i, 0)
    fixed = lambda i: (0, 0)
    return pl.pallas_call(
        functools.partial(_rwkv_post_kernel, dh=dh),
        grid=(n // tm,),
        in_specs=[pl.BlockSpec((tm, wa), row)] * 3 + [pl.BlockSpec((1, wa), fixed)] * 2
                 + [pl.BlockSpec((wa, wa), fixed)],
        out_specs=pl.BlockSpec((tm, wa), row),
        out_shape=jax.ShapeDtypeStruct((n, wa), BF16),
        compiler_params=_params(("parallel",)),
        name="rwkv_post",
    )(y, bonus, g, gn_g.reshape(1, wa), gn_b.reshape(1, wa), seg)


def _state_to_pairs(s):
    nb, h, dh, _ = s.shape
    return s.reshape(nb, h // 2, 2, dh, dh).transpose(0, 1, 3, 2, 4).reshape(nb, h // 2, dh, 2 * dh)


def _state_from_pairs(s, h, dh):
    nb = s.shape[0]
    return s.reshape(nb, h // 2, dh, 2, dh).transpose(0, 1, 3, 2, 4).reshape(nb, h, dh, dh)


def _suffix_matrix(n):
    r = lax.broadcasted_iota(I32, (n, n), 0)
    c = lax.broadcasted_iota(I32, (n, n), 1)
    return (r > c).astype(BF16)


def _sb_block(qm, kb, vm, mask, carry, suffix, scale):
    z = _dot_nt(qm, kb) * scale
    t = jnp.log1p(jnp.exp(-jnp.abs(z)))
    log_keep = jnp.where(mask, -(jnp.maximum(z, 0.0) + t), 0.0)
    log_beta = -(jnp.maximum(-z, 0.0) + t)
    between = _dot_exact_rhs01(log_keep, suffix) + carry
    wgt = jnp.where(mask, jnp.exp(log_beta + between), 0.0)
    carry = carry + jnp.sum(log_keep, axis=-1, keepdims=True)
    return _dot(wgt.astype(BF16), vm), carry


def _sb_prompt_kernel(q_ref, k_ref, v_ref, o_ref, *, tq, dh, scale):
    qi = pl.program_id(2)
    lane = lax.broadcasted_iota(I32, (1, LANES), 1)
    heads = [(lane // dh) == hh for hh in range(LANES // dh)]
    q = q_ref[...]
    qms = [jnp.where(hm, q, 0.0).astype(BF16) for hm in heads]
    suffix = _suffix_matrix(tq)
    r = lax.broadcasted_iota(I32, (tq, tq), 0)
    c = lax.broadcasted_iota(I32, (tq, tq), 1)
    strictly_before = c < r

    def body(it, state):
        carries, acc = state
        start = pl.multiple_of((qi - it) * tq, tq)
        kb = k_ref[pl.ds(start, tq), :].astype(BF16)
        vb = v_ref[pl.ds(start, tq), :]
        mask = jnp.logical_or(strictly_before, it > 0)
        new_carries = []
        for hm, qm, carry in zip(heads, qms, carries):
            vm = jnp.where(hm, vb, 0.0).astype(BF16)
            contrib, carry = _sb_block(qm, kb, vm, mask, carry, suffix, scale)
            acc = acc + contrib
            new_carries.append(carry)
        return tuple(new_carries), acc

    init = (tuple(jnp.zeros((tq, 1), F32) for _ in heads), jnp.zeros((tq, LANES), F32))
    _, acc = lax.fori_loop(0, qi + 1, body, init)
    o_ref[...] = acc.astype(o_ref.dtype)


def _sb_prompt(q, k, v, nb, t_len, dh):
    w = q.shape[1]
    tq = _pick(t_len, (128,))
    nq = t_len // tq
    return pl.pallas_call(
        functools.partial(_sb_prompt_kernel, tq=tq, dh=dh, scale=dh ** -0.5),
        grid=(nb, w // LANES, nq),
        in_specs=[pl.BlockSpec((tq, LANES), lambda b, h, i: (b * nq + i, h)),
                  pl.BlockSpec((t_len, LANES), lambda b, h, i: (b, h)),
                  pl.BlockSpec((t_len, LANES), lambda b, h, i: (b, h))],
        out_specs=pl.BlockSpec((tq, LANES), lambda b, h, i: (b * nq + i, h)),
        out_shape=jax.ShapeDtypeStruct((nb * t_len, w), BF16),
        compiler_params=_params(("parallel", "parallel", "parallel")),
        name="sb_attn_prompt",
    )(q, k, v)


def _page_of_step(j, n_pages):
    return n_pages - jnp.maximum(j, 1)


def _sample_blocks(j, kn_ref, vn_ref, kp_ref, vp_ref, *, dt, page, past_len, n_pages):
    is_new = j == 0
    w = kn_ref.shape[-1]
    pad = jnp.zeros((page - dt, w), F32)
    kblk = jnp.where(is_new, jnp.concatenate([kn_ref[...], pad], axis=0), kp_ref[...])
    vblk = jnp.where(is_new, jnp.concatenate([vn_ref[...], pad], axis=0), vp_ref[...])
    base = jnp.where(is_new, past_len, _page_of_step(j, n_pages) * page)
    qrow = lax.broadcasted_iota(I32, (dt, page), 0)
    kcol = lax.broadcasted_iota(I32, (dt, page), 1)
    qpos = past_len + qrow
    kpos = base + kcol
    live = jnp.logical_or(kcol < dt, jnp.logical_not(is_new))
    return kblk, vblk, qpos, kpos, live


def _sb_sample_kernel(pt_ref, q_ref, kn_ref, vn_ref, kp_ref, vp_ref, o_ref, carry_ref,
                      *, dt, page, past_len, n_pages, dh, scale):
    j = pl.program_id(1)

    @pl.when(j == 0)
    def _():
        o_ref[...] = jnp.zeros(o_ref.shape, F32)
        carry_ref[...] = jnp.zeros(carry_ref.shape, F32)

    kblk, vblk, qpos, kpos, live = _sample_blocks(j, kn_ref, vn_ref, kp_ref, vp_ref,
                                                  dt=dt, page=page, past_len=past_len, n_pages=n_pages)
    mask = jnp.logical_and(kpos < qpos, live)
    suffix = _suffix_matrix(page)
    lane = lax.broadcasted_iota(I32, (1, LANES), 1)
    hpb = LANES // dh
    for hp in range(q_ref.shape[-1] // LANES):
        cs = slice(hp * LANES, (hp + 1) * LANES)
        qp = q_ref[:, cs]
        kb = kblk[:, cs].astype(BF16)
        vb = vblk[:, cs]
        acc = jnp.zeros((dt, LANES), F32)
        for hh in range(hpb):
            hm = (lane // dh) == hh
            qm = jnp.where(hm, qp, 0.0).astype(BF16)
            vm = jnp.where(hm, vb, 0.0).astype(BF16)
            carry = carry_ref[hp * hpb + hh][:, 0:1]
            contrib, carry = _sb_block(qm, kb, vm, mask, carry, suffix, scale)
            carry_ref[hp * hpb + hh] = jnp.broadcast_to(carry, (dt, LANES))
            acc = acc + contrib
        o_ref[:, cs] += acc


def _paged_specs(dt, w_q, w_k, w_v, page, n_pages, row_blk0):
    new = lambda w: pl.BlockSpec((dt, w), lambda b, j, pt: (row_blk0 + b, 0))
    pool = lambda w: pl.BlockSpec((None, page, w), lambda b, j, pt: (pt[b, _page_of_step(j, n_pages)], 0, 0))
    return [new(w_q), new(w_k), new(w_v), pool(w_k), pool(w_v)]


def _sb_sample(page_table, q, k, v, kpool, vpool, row0, nb, dt, dh):
    w = q.shape[1]
    n_pages = page_table.shape[1]
    page = kpool.shape[1]
    past_len = n_pages * page
    kern = functools.partial(_sb_sample_kernel, dt=dt, page=page, past_len=past_len, n_pages=n_pages,
                             dh=dh, scale=dh ** -0.5)
    return pl.pallas_call(
        kern,
        grid_spec=pltpu.PrefetchScalarGridSpec(
            num_scalar_prefetch=1,
            grid=(nb, n_pages + 1),
            in_specs=_paged_specs(dt, w, w, w, page, n_pages, row0 // dt),
            out_specs=pl.BlockSpec((dt, w), lambda b, j, pt: (b, 0)),
            scratch_shapes=[pltpu.VMEM((w // dh, dt, LANES), F32)]),
        out_shape=jax.ShapeDtypeStruct((nb * dt, w), F32),
        compiler_params=_params(("parallel", "arbitrary")),
        name="sb_attn_sample",
    )(page_table, q, k, v, kpool, vpool)


def _t5_bias(n, rb_ref, h):
    max_exact = NUM_BUCKETS // 2
    nf = jnp.maximum(n, 1).astype(F32)
    far = max_exact + (jnp.log(nf / max_exact) / math.log(MAX_DISTANCE / max_exact)
                       * (NUM_BUCKETS - max_exact)).astype(I32)
    bucket = jnp.where(n < max_exact, n, jnp.minimum(far, NUM_BUCKETS - 1))
    bias = jnp.full(n.shape, rb_ref[0, h], F32)
    for b in range(1, NUM_BUCKETS):
        bias = jnp.where(bucket == b, rb_ref[b, h], bias)
    return bias


def _softmax_block(qm, kb, vb, bias, mask, state, scale):
    m, l, acc = state
    s = jnp.where(mask, _dot_nt(qm, kb) * scale + bias, NEG_INF)
    m_new = jnp.maximum(m, jnp.max(s, axis=-1, keepdims=True))
    p = jnp.where(mask, jnp.exp(s - m_new), 0.0)
    alpha = jnp.exp(m - m_new)
    l = alpha * l + jnp.sum(p, axis=-1, keepdims=True)
    acc = alpha * acc + _dot(p.astype(BF16), vb)
    return m_new, l, acc


def _diff_lambda(lq1_ref, lk1_ref, lq2_ref, lk2_ref, lam_init):
    s1 = jnp.sum(lq1_ref[...] * lk1_ref[...], axis=-1, keepdims=True)
    s2 = jnp.sum(lq2_ref[...] * lk2_ref[...], axis=-1, keepdims=True)
    return jnp.exp(s1) - jnp.exp(s2) + lam_init


def _diff_finish(states, lam, g, lam_init):
    (_, l0, a0), (_, l1, a1) = states
    o = a0 / l0 - lam * (a1 / l1)
    return o * lax.rsqrt(jnp.mean(o * o, -1, keepdims=True) + SUBLN_EPS) * g * (1.0 - lam_init)


def _diff_prompt_kernel(rb_ref, q_ref, k_ref, v_ref, lq1_ref, lk1_ref, lq2_ref, lk2_ref, g_ref, o_ref,
                        *, tq, dh, scale, lam_init):
    h = pl.program_id(1)
    qi = pl.program_id(2)
    lane = lax.broadcasted_iota(I32, (1, LANES), 1)
    q = q_ref[...]
    qms = [jnp.where((lane // dh) == br, q, 0.0).astype(BF16) for br in range(2)]
    r = lax.broadcasted_iota(I32, (tq, tq), 0)
    c = lax.broadcasted_iota(I32, (tq, tq), 1)

    def body(kj, states):
        start = pl.multiple_of(kj * tq, tq)
        kb = k_ref[pl.ds(start, tq), :].astype(BF16)
        vb = v_ref[pl.ds(start, tq), :].astype(BF16)
        dist = (qi - kj) * tq + r - c
        mask = dist >= 0
        bias = _t5_bias(jnp.maximum(dist, 0), rb_ref, h)
        return tuple(_softmax_block(qm, kb, vb, bias, mask, st, scale) for qm, st in zip(qms, states))

    st0 = (jnp.full((tq, 1), NEG_INF, F32), jnp.zeros((tq, 1), F32), jnp.zeros((tq, LANES), F32))
    states = lax.fori_loop(0, qi + 1, body, (st0, st0))
    lam = _diff_lambda(lq1_ref, lk1_ref, lq2_ref, lk2_ref, lam_init)
    o_ref[...] = _diff_finish(states, lam, g_ref[...], lam_init).astype(o_ref.dtype)


def _lam_specs(dh, index_map):
    return [pl.BlockSpec((1, dh), index_map)] * 4 + [pl.BlockSpec((1, 2 * dh), index_map)]


def _diff_prompt(rel_bias, q, k, v, lams, subln_g, nb, t_len, dh, lam_init):
    w = q.shape[1]
    assert 2 * dh == LANES
    tq = _pick(t_len, (128,))
    nq = t_len // tq
    fixed = lambda b, h, i: (0, 0)
    kern = functools.partial(_diff_prompt_kernel, tq=tq, dh=dh, scale=dh ** -0.5, lam_init=lam_init)
    return pl.pallas_call(
        kern,
        grid=(nb, w // LANES, nq),
        in_specs=[pl.BlockSpec(memory_space=pltpu.SMEM),
                  pl.BlockSpec((tq, LANES), lambda b, h, i: (b * nq + i, h)),
                  pl.BlockSpec((t_len, LANES), lambda b, h, i: (b, h)),
                  pl.BlockSpec((t_len, LANES), lambda b, h, i: (b, h))] + _lam_specs(dh, fixed),
        out_specs=pl.BlockSpec((tq, LANES), lambda b, h, i: (b * nq + i, h)),
        out_shape=jax.ShapeDtypeStruct((nb * t_len, w), BF16),
        compiler_params=_params(("parallel", "parallel", "parallel")),
        name="diff_attn_prompt",
    )(rel_bias, q, k, v, *lams, subln_g)


def _diff_sample_kernel(pt_ref, rb_ref, q_ref, kn_ref, vn_ref, kp_ref, vp_ref,
                        lq1_ref, lk1_ref, lq2_ref, lk2_ref, g_ref, o_ref, m_ref, l_ref, acc_ref,
                        *, dt, page, past_len, n_pages, dh, scale, lam_init):
    j = pl.program_id(1)

    @pl.when(j == 0)
    def _():
        m_ref[...] = jnp.full(m_ref.shape, NEG_INF, F32)
        l_ref[...] = jnp.zeros(l_ref.shape, F32)
        acc_ref[...] = jnp.zeros(acc_ref.shape, F32)

    kblk, vblk, qpos, kpos, live = _sample_blocks(j, kn_ref, vn_ref, kp_ref, vp_ref,
                                                  dt=dt, page=page, past_len=past_len, n_pages=n_pages)
    dist = qpos - kpos
    mask = jnp.logical_and(dist >= 0, live)
    dist = jnp.maximum(dist, 0)
    lane = lax.broadcasted_iota(I32, (1, LANES), 1)
    n_heads = q_ref.shape[-1] // LANES
    for h in range(n_heads):
        cs = slice(h * LANES, (h + 1) * LANES)
        qp = q_ref[:, cs]
        kb = kblk[:, cs].astype(BF16)
        vb = vblk[:, cs].astype(BF16)
        bias = _t5_bias(dist, rb_ref, h)
        for br in range(2):
            qm = jnp.where((lane // dh) == br, qp, 0.0).astype(BF16)
            st = (m_ref[2 * h + br][:, 0:1], l_ref[2 * h + br][:, 0:1], acc_ref[br, :, cs])
            m, l, acc = _softmax_block(qm, kb, vb, bias, mask, st, scale)
            m_ref[2 * h + br] = jnp.broadcast_to(m, (dt, LANES))
            l_ref[2 * h + br] = jnp.broadcast_to(l, (dt, LANES))
            acc_ref[br, :, cs] = acc

    @pl.when(j == n_pages)
    def _():
        lam = _diff_lambda(lq1_ref, lk1_ref, lq2_ref, lk2_ref, lam_init)
        for h in range(n_heads):
            cs = slice(h * LANES, (h + 1) * LANES)
            states = tuple((None, l_ref[2 * h + br][:, 0:1], acc_ref[br, :, cs]) for br in range(2))
            o_ref[:, cs] = _diff_finish(states, lam, g_ref[...], lam_init)


def _diff_sample(page_table, rel_bias, q, k, v, kpool, vpool, lams, subln_g, row0, nb, dt, dh, lam_init):
    w = q.shape[1]
    assert 2 * dh == LANES
    n_pages = page_table.shape[1]
    page = kpool.shape[1]
    past_len = n_pages * page
    fixed = lambda b, j, pt: (0, 0)
    kern = functools.partial(_diff_sample_kernel, dt=dt, page=page, past_len=past_len, n_pages=n_pages,
                             dh=dh, scale=dh ** -0.5, lam_init=lam_init)
    return pl.pallas_call(
        kern,
        grid_spec=pltpu.PrefetchScalarGridSpec(
            num_scalar_prefetch=1,
            grid=(nb, n_pages + 1),
            in_specs=[pl.BlockSpec(memory_space=pltpu.SMEM)]
                     + _paged_specs(dt, w, w, w, page, n_pages, row0 // dt) + _lam_specs(dh, fixed),
            out_specs=pl.BlockSpec((dt, w), lambda b, j, pt: (b, 0)),
            scratch_shapes=[pltpu.VMEM((2 * w // LANES, dt, LANES), F32),
                            pltpu.VMEM((2 * w // LANES, dt, LANES), F32),
                            pltpu.VMEM((2, dt, w), F32)]),
        out_shape=jax.ShapeDtypeStruct((nb * dt, w), F32),
        compiler_params=_params(("parallel", "arbitrary")),
        name="diff_attn_sample",
    )(page_table, rel_bias, q, k, v, kpool, vpool, *lams, subln_g)


def _router_kernel(h_ref, wr_ref, idx_ref, gate_ref, cnt_ref, carry_ref, *, n_experts):
    i = pl.program_id(0)

    @pl.when(i == 0)
    def _():
        carry_ref[...] = jnp.zeros(carry_ref.shape, F32)

    tm = h_ref.shape[0]
    hh, hm, hl = _split3(h_ref[...])
    wh, wm, wl = _split3(wr_ref[...])
    logits = (_dot(hh, wh) + (_dot(hh, wm) + _dot(hm, wh))
              + (_dot(hh, wl) + _dot(hm, wm) + _dot(hl, wh)))
    lane = lax.broadcasted_iota(I32, (tm, LANES), 1)
    lg = jnp.where(lane < n_experts, logits, -jnp.inf)
    m1 = jnp.max(lg, axis=-1, keepdims=True)
    i1 = jnp.min(jnp.where(lg == m1, lane, LANES), axis=-1, keepdims=True)
    lg2 = jnp.where(lane == i1, -jnp.inf, lg)
    m2 = jnp.max(lg2, axis=-1, keepdims=True)
    i2 = jnp.min(jnp.where(lg2 == m2, lane, LANES), axis=-1, keepdims=True)
    e2 = jnp.exp(m2 - m1)
    denom = 1.0 + e2
    g1 = 1.0 / denom
    g2 = e2 / denom

    r = lax.broadcasted_iota(I32, (tm, tm), 0)
    c = lax.broadcasted_iota(I32, (tm, tm), 1)
    before = (c < r).astype(BF16)
    ranks = []
    carry = carry_ref[...]
    for sel in (i1, i2):
        onehot = lane == sel
        inside = _dot(before, onehot.astype(BF16))
        ranks.append(jnp.sum(jnp.where(onehot, carry + inside, 0.0), axis=-1, keepdims=True))
        carry = carry + jnp.sum(onehot.astype(F32), axis=0, keepdims=True)
    carry_ref[...] = carry
    cnt_ref[...] = carry
    idx = jnp.where(lane == 0, i1, jnp.where(lane == 1, i2, 0))
    idx = jnp.where(lane == 2, ranks[0].astype(I32), jnp.where(lane == 3, ranks[1].astype(I32), idx))
    idx_ref[...] = idx
    gate_ref[...] = jnp.where(lane == 0, g1, jnp.where(lane == 1, g2, 0.0))


def _router(h, router):
    n, d = h.shape
    n_experts = router.shape[1]
    wr = jnp.zeros((d, LANES), F32).at[:, :n_experts].set(router)
    tm = _pick(n, (512, 256, 128))
    return pl.pallas_call(
        functools.partial(_router_kernel, n_experts=n_experts),
        grid=(n // tm,),
        in_specs=[pl.BlockSpec((tm, d), lambda i: (i, 0)), pl.BlockSpec((d, LANES), lambda i: (0, 0))],
        out_specs=[pl.BlockSpec((tm, LANES), lambda i: (i, 0)), pl.BlockSpec((tm, LANES), lambda i: (i, 0)),
                   pl.BlockSpec((1, LANES), lambda i: (0, 0))],
        out_shape=[jax.ShapeDtypeStruct((n, LANES), I32), jax.ShapeDtypeStruct((n, LANES), F32),
                   jax.ShapeDtypeStruct((1, LANES), F32)],
        scratch_shapes=[pltpu.VMEM((1, LANES), F32)],
        compiler_params=_params(("arbitrary",)),
        name="moe_router",
    )(h, wr)


def _row_copy(src, dst, s, d, sem):
    return pltpu.make_async_copy(src.at[pl.ds(s, 1)], dst.at[pl.ds(d, 1)], sem)


def _dispatch_kernel(pos_ref, h_ref, init_ref, o_ref, sem, *, tm):
    del init_ref

    def start(r, _):
        for k in range(TOP_K):
            _row_copy(h_ref, o_ref, r, pos_ref[0, 0, TOP_K * r + k], sem).start()
        return 0

    def wait(r, _):
        for k in range(TOP_K):
            _row_copy(h_ref, o_ref, 0, 0, sem).wait()
        return 0

    lax.fori_loop(0, tm, start, 0)
    lax.fori_loop(0, tm, wait, 0)


def _dispatch(h, pos, n_rows):
    n, d = h.shape
    tm = _pick(n, (256, 128))
    pos3 = pos.reshape(n // tm, 1, tm * TOP_K)
    return pl.pallas_call(
        functools.partial(_dispatch_kernel, tm=tm),
        grid=(n // tm,),
        in_specs=[pl.BlockSpec((1, 1, tm * TOP_K), lambda i: (i, 0, 0), memory_space=pltpu.SMEM),
                  pl.BlockSpec((tm, d), lambda i: (i, 0)),
                  pl.BlockSpec(memory_space=pl.ANY)],
        out_specs=pl.BlockSpec(memory_space=pl.ANY),
        out_shape=jax.ShapeDtypeStruct((n_rows, d), h.dtype),
        scratch_shapes=[pltpu.SemaphoreType.DMA],
        input_output_aliases={2: 0},
        compiler_params=_params(("arbitrary",)),
        name="moe_dispatch",
    )(pos3, h, jnp.zeros((n_rows, d), h.dtype))


def _moe_kernel(te_ref, nt_ref, x_ref, wg_ref, wu_ref, wd_ref, init_ref, y_ref, xb_ref):
    del init_ref
    t = pl.program_id(0)
    j = pl.program_id(1)

    @pl.when(t < nt_ref[0])
    def _():
        @pl.when(j == 0)
        def _():
            xb_ref[...] = x_ref[...].astype(BF16)

        contrib = _swiglu_tile(xb_ref[...], wg_ref[...], wu_ref[...], wd_ref[...])

        @pl.when(j == 0)
        def _():
            y_ref[...] = contrib

        @pl.when(j > 0)
        def _():
            y_ref[...] += contrib


def _moe_ffn(x_sorted, tile_expert, n_tiles, wg, wu, wd, tm):
    n_rows, d = x_sorted.shape
    f = wg.shape[2]
    tf = _pick(f, (512, 256, 128))
    nf = f // tf
    max_tiles = n_rows // tm

    def live(t, j, nt):
        ok = t < nt[0]
        return jnp.minimum(t, nt[0] - 1), jnp.where(ok, j, nf - 1)

    def x_map(t, j, te, nt):
        tt, _ = live(t, j, nt)
        return tt, 0

    def w_map(t, j, te, nt):
        tt, jj = live(t, j, nt)
        return te[tt], 0, jj

    def wd_map(t, j, te, nt):
        tt, jj = live(t, j, nt)
        return te[tt], jj, 0

    return pl.pallas_call(
        _moe_kernel,
        grid_spec=pltpu.PrefetchScalarGridSpec(
            num_scalar_prefetch=2,
            grid=(max_tiles, nf),
            in_specs=[pl.BlockSpec((tm, d), x_map),
                      pl.BlockSpec((None, d, tf), w_map),
                      pl.BlockSpec((None, d, tf), w_map),
                      pl.BlockSpec((None, tf, d), wd_map),
                      pl.BlockSpec(memory_space=pl.ANY)],
            out_specs=pl.BlockSpec((tm, d), x_map),
            scratch_shapes=[pltpu.VMEM((tm, d), BF16)]),
        out_shape=jax.ShapeDtypeStruct((n_rows, d), F32),
        input_output_aliases={6: 0},
        compiler_params=_params(("arbitrary", "arbitrary")),
        name="moe_ffn",
    )(tile_expert, n_tiles, x_sorted, wg, wu, wd, jnp.zeros((n_rows, d), F32))


def _combine_ln_kernel(pos_ref, gate_ref, h_ref, e_ref, g_ref, b_ref, ys_ref, y_ref, buf_ref, sem, *, tm, alpha):
    def start(r, _):
        for k in range(TOP_K):
            _row_copy(ys_ref, buf_ref.at[k], pos_ref[0, 0, TOP_K * r + k], r, sem).start()
        return 0

    def wait(r, _):
        for k in range(TOP_K):
            _row_copy(ys_ref, buf_ref.at[k], 0, 0, sem).wait()
        return 0

    lax.fori_loop(0, tm, start, 0)
    lax.fori_loop(0, tm, wait, 0)
    gates = gate_ref[...]
    f = gates[:, 0:1] * buf_ref[0] + gates[:, 1:2] * buf_ref[1]
    z = alpha * h_ref[...] + f + e_ref[...]
    y_ref[...] = _layer_norm(z, g_ref[...], b_ref[...])


def _combine_ln(pos, gates, h, e, g, b, y_sorted, alpha):
    n, d = h.shape
    tm = _pick(n, (256, 128))
    pos3 = pos.reshape(n // tm, 1, tm * TOP_K)
    row = lambda i: (i, 0)
    fixed = lambda i: (0, 0)
    return pl.pallas_call(
        functools.partial(_combine_ln_kernel, tm=tm, alpha=alpha),
        grid=(n // tm,),
        in_specs=[pl.BlockSpec((1, 1, tm * TOP_K), lambda i: (i, 0, 0), memory_space=pltpu.SMEM),
                  pl.BlockSpec((tm, LANES), row), pl.BlockSpec((tm, d), row), pl.BlockSpec((tm, d), row),
                  pl.BlockSpec((1, d), fixed), pl.BlockSpec((1, d), fixed),
                  pl.BlockSpec(memory_space=pl.ANY)],
        out_specs=pl.BlockSpec((tm, d), row),
        out_shape=jax.ShapeDtypeStruct((n, d), F32),
        scratch_shapes=[pltpu.VMEM((TOP_K, tm, d), F32), pltpu.SemaphoreType.DMA],
        compiler_params=_params(("arbitrary",)),
        name="moe_combine_ln",
    )(pos3, gates, h, e, g.reshape(1, d), b.reshape(1, d), y_sorted)


def _moe_layout(idx, counts, n_experts, tm):
    n = idx.shape[0]
    max_tiles = (n * TOP_K) // tm + n_experts
    cnt = counts[0, :n_experts].astype(I32)
    tiles = (cnt + tm - 1) // tm
    tile_end = jnp.cumsum(tiles)
    row_start = (tile_end - tiles) * tm
    pos = row_start[idx[:, 0:TOP_K]] + idx[:, TOP_K:2 * TOP_K]
    tile_expert = jnp.minimum(jnp.searchsorted(tile_end, jnp.arange(max_tiles, dtype=I32), side="right"),
                              n_experts - 1).astype(I32)
    return pos.astype(I32), tile_expert, tile_end[-1:].astype(I32), max_tiles * tm


def _segment_matrix(w, dh):
    i = jnp.arange(w) // dh
    return (i[:, None] == i[None, :]).astype(BF16)


def _shifted(u, shift_prompt, shift_sample, nb, t_len, db, dt):
    pa = u.shape[1]
    up = u[:nb * t_len].reshape(nb, t_len, pa)
    us = u[nb * t_len:].reshape(db, dt, pa)
    pp = jnp.concatenate([shift_prompt[:, None, :], up[:, :-1]], axis=1).reshape(nb * t_len, pa)
    ps = jnp.concatenate([shift_sample[:, None, :], us[:, :-1]], axis=1).reshape(db * dt, pa)
    return jnp.concatenate([pp, ps], axis=0)


def kernel(x_prompt, x_sample, cache_sb_k, cache_sb_v, cache_diff_k, cache_diff_v, state_rwkv, state_rwkv_shift, page_table, p_prompt, p_sample, w_in_even, w_out_even, rwkv_mu, rwkv_w0, rwkv_w2, rwkv_a0, rwkv_a2, rwkv_g2, rwkv_k_k, rwkv_k_a, rwkv_r_k, rwkv_gn_g, rwkv_gn_b, w_in_odd, w_out_odd, diff_lq1, diff_lk1, diff_lq2, diff_lk2, diff_subln_g, rel_bias, ffn_gate, ffn_up, ffn_down, moe_router, moe_gate, moe_up, moe_down, ple_gate, ple_proj, ln1_g, ln1_b, ln2_g, ln2_b):
    nb, t_len, d = x_prompt.shape
    db, dt, _ = x_sample.shape
    depth = ple_gate.shape[0]
    alpha = (2 * depth) ** 0.25
    n_prompt = nb * t_len
    n_sample = db * dt
    h_a, dh_a = state_rwkv.shape[2], state_rwkv.shape[3]
    w_a = h_a * dh_a
    p_a = rwkv_mu.shape[1]
    lora_w, lora_a = rwkv_w2.shape[1], rwkv_a2.shape[1]
    h_b, dh_b = cache_sb_k.shape[3], cache_sb_k.shape[4]
    w_b = h_b * dh_b
    h_c, dh_c = cache_diff_v.shape[3], cache_diff_k.shape[4]
    w_c = 2 * h_c * dh_c
    n_pool, page = cache_sb_k.shape[1], cache_sb_k.shape[2]
    n_experts = moe_router.shape[2]
    assert lora_w + lora_a == LANES and dh_a * 2 == LANES

    x = jnp.concatenate([x_prompt.reshape(n_prompt, d), x_sample.reshape(n_sample, d)], axis=0)
    xb = x.astype(BF16)
    seg = _segment_matrix(w_a, dh_a)

    sb_k, sb_v, dk, dv, st_p, st_s, sh_p, sh_s = [], [], [], [], [], [], [], []
    for i in range(depth):
        j = i // 2
        p_i = jnp.concatenate([p_prompt[i].reshape(n_prompt, -1), p_sample[i].reshape(n_sample, -1)],
                              axis=0).astype(BF16)
        if i % 2 == 0:
            proj = _matmul(xb, w_in_even[j].astype(BF16))
            u = proj[:, :p_a]
            q_b = proj[:, p_a:p_a + w_b]
            k_b = proj[:, p_a + w_b:p_a + 2 * w_b]
            v_b = proj[:, p_a + 2 * w_b:]

            prev = _shifted(u, jnp.zeros((nb, p_a), F32), state_rwkv_shift[j], nb, t_len, db, dt)
            w2p = jnp.concatenate([rwkv_w2[j], jnp.zeros((lora_a, w_a), F32)], axis=0).astype(BF16)
            a2p = jnp.concatenate([jnp.zeros((lora_w, w_a), F32), rwkv_a2[j]], axis=0).astype(BF16)
            r, wdec, k2, v, a_t, b_t, g, bonus = _rwkv_prep(
                u, prev, rwkv_mu[j], rwkv_w0[j], rwkv_a0[j], rwkv_k_k[j], rwkv_k_a[j], rwkv_r_k[j].reshape(-1),
                w2p, a2p, rwkv_g2[j].astype(BF16), seg)
            seqs = (r, wdec, k2, v, a_t, b_t)
            tc = 8
            y_p, s_p = _rwkv_scan(seqs, jnp.zeros((nb, h_a // 2, dh_a, LANES), F32), 0, nb, t_len, tc, dh_a)
            y_s, s_s = _rwkv_scan(seqs, _state_to_pairs(state_rwkv[j]), n_prompt, db, dt, dt, dh_a)
            o_a = _rwkv_post(jnp.concatenate([y_p, y_s], axis=0), bonus, g, rwkv_gn_g[j], rwkv_gn_b[j], seg, dh_a)
            st_p.append(_state_from_pairs(s_p, h_a, dh_a))
            st_s.append(_state_from_pairs(s_s, h_a, dh_a))
            sh_p.append(u[:n_prompt].reshape(nb, t_len, p_a)[:, -1])
            sh_s.append(u[n_prompt:].reshape(db, dt, p_a)[:, -1])

            ob_p = _sb_prompt(q_b, k_b, v_b, nb, t_len, dh_b)
            ob_s = _sb_sample(page_table, q_b, k_b, v_b,
                              cache_sb_k[j].reshape(n_pool, page, w_b), cache_sb_v[j].reshape(n_pool, page, w_b),
                              n_prompt, db, dt, dh_b)
            o_b = jnp.concatenate([ob_p, ob_s.astype(BF16)], axis=0)
            sb_k.append(k_b)
            sb_v.append(v_b)

            mix_in = jnp.concatenate([o_a, o_b], axis=1)
            h, hb = _matmul_residual_ln(mix_in, w_out_even[j].astype(BF16), x, ln1_g[i], ln1_b[i], alpha)
            f = _ffn(hb, ffn_gate[j].astype(BF16), ffn_up[j].astype(BF16), ffn_down[j].astype(BF16))
            e = _per_layer_embed(hb, ple_gate[i].astype(BF16), p_i, ple_proj[i].astype(BF16))
            x, xb = _add_ln(h, f, e, ln2_g[i], ln2_b[i], alpha)
        else:
            proj = _matmul(xb, w_in_odd[j].astype(BF16))
            hd = h_c * dh_c
            q = proj[:, :2 * hd]
            k = proj[:, 2 * hd:4 * hd]
            v = proj[:, 4 * hd:]
            lam_init = 0.8 - 0.6 * math.exp(-0.3 * i)
            lams = [t[j].reshape(1, dh_c) for t in (diff_lq1, diff_lk1, diff_lq2, diff_lk2)]
            g_sub = diff_subln_g[j].reshape(1, 2 * dh_c)
            o_p = _diff_prompt(rel_bias, q, k, v, lams, g_sub, nb, t_len, dh_c, lam_init)
            o_s = _diff_sample(page_table, rel_bias, q, k, v,
                               cache_diff_k[j].reshape(n_pool, page, w_c), cache_diff_v[j].reshape(n_pool, page, w_c),
                               lams, g_sub, n_prompt, db, dt, dh_c, lam_init)
            o = jnp.concatenate([o_p, o_s.astype(BF16)], axis=0)
            dk.append(k)
            dv.append(v)

            h, hb = _matmul_residual_ln(o, w_out_odd[j].astype(BF16), x, ln1_g[i], ln1_b[i], alpha)
            idx, gates, counts = _router(h, moe_router[j])
            tm_moe = 512 if (h.shape[0] * TOP_K) % 512 == 0 else 128
            pos, tile_expert, n_tiles, n_rows = _moe_layout(idx, counts, n_experts, tm_moe)
            x_sorted = _dispatch(h, pos.reshape(-1), n_rows)
            y_sorted = _moe_ffn(x_sorted, tile_expert, n_tiles, moe_gate[j].astype(BF16), moe_up[j].astype(BF16),
                                moe_down[j].astype(BF16), tm_moe)
            e = _per_layer_embed(hb, ple_gate[i].astype(BF16), p_i, ple_proj[i].astype(BF16))
            x = _combine_ln(pos.reshape(-1), gates, h, e, ln2_g[i], ln2_b[i], y_sorted, alpha)
            xb = x.astype(BF16)

    def split(t, shape_p, shape_s):
        return t[:n_prompt].reshape(shape_p), t[n_prompt:].reshape(shape_s)

    y_p, y_s = split(x, (nb, t_len, d), (db, dt, d))
    kb = [split(t, (nb, t_len, h_b, dh_b), (db, dt, h_b, dh_b)) for t in sb_k]
    vb = [split(t, (nb, t_len, h_b, dh_b), (db, dt, h_b, dh_b)) for t in sb_v]
    kc = [split(t, (nb, t_len, 2 * h_c, dh_c), (db, dt, 2 * h_c, dh_c)) for t in dk]
    vc = [split(t, (nb, t_len, h_c, 2 * dh_c), (db, dt, h_c, 2 * dh_c)) for t in dv]
    stack = lambda parts, which: jnp.stack([p[which] for p in parts])
    return (y_p, y_s,
            stack(kb, 0), stack(vb, 0), stack(kc, 0), stack(vc, 0), jnp.stack(st_p), jnp.stack(sh_p),
            stack(kb, 1), stack(vb, 1), stack(kc, 1), stack(vc, 1), jnp.stack(st_s), jnp.stack(sh_s))
```

```python
import functools
import math

import jax
import jax.numpy as jnp
from jax import lax
from jax.experimental import pallas as pl
from jax.experimental.pallas import tpu as pltpu

F32 = jnp.float32
BF16 = jnp.bfloat16
I32 = jnp.int32

LANES = 128
VMEM_LIMIT = 56 * 1024 * 1024

LN_EPS = 1e-5
SUBLN_EPS = 1e-5
GN_EPS_A = 64e-5
NEG_INF = -1e30
NUM_BUCKETS = 32
MAX_DISTANCE = 128
TOP_K = 2


def _pick(n, cands):
    for c in cands:
        if n % c == 0:
            return c
    raise ValueError(f"no tile in {cands} divides {n}")


def _params(sem):
    return pltpu.CompilerParams(dimension_semantics=sem, vmem_limit_bytes=VMEM_LIMIT)


def _dot(a, b):
    return jnp.dot(a, b, preferred_element_type=F32)


def _dot_nt(a, b):
    return lax.dot_general(a, b, (((1,), (1,)), ((), ())), preferred_element_type=F32)


def _split3(x):
    hi = x.astype(BF16)
    r1 = x - hi.astype(F32)
    mid = r1.astype(BF16)
    lo = (r1 - mid.astype(F32)).astype(BF16)
    return hi, mid, lo


def _dot_exact_rhs01(x, m01):
    hi, mid, lo = _split3(x)
    return _dot(hi, m01) + _dot(mid, m01) + _dot(lo, m01)


def _dot_2piece_rhs01(x, m01):
    hi = x.astype(BF16)
    lo = (x - hi.astype(F32)).astype(BF16)
    return _dot(hi, m01) + _dot(lo, m01)


def _layer_norm(z, g, b):
    mu = jnp.mean(z, -1, keepdims=True)
    d = z - mu
    var = jnp.mean(d * d, -1, keepdims=True)
    return d * lax.rsqrt(var + LN_EPS) * g + b


def _sigmoid(x):
    return 1.0 / (1.0 + jnp.exp(-x))


def _mm_kernel(x_ref, w_ref, o_ref):
    o_ref[...] = _dot(x_ref[...], w_ref[...]).astype(o_ref.dtype)


def _matmul(x, w, out_dtype=F32):
    m, k = x.shape
    n = w.shape[1]
    tm = _pick(m, (1024, 512, 256, 128))
    tn = _pick(n, (1280, 1024, 768, 512, 256, 128))
    return pl.pallas_call(
        _mm_kernel,
        grid=(m // tm, n // tn),
        in_specs=[pl.BlockSpec((tm, k), lambda i, j: (i, 0)),
                  pl.BlockSpec((k, tn), lambda i, j: (0, j))],
        out_specs=pl.BlockSpec((tm, tn), lambda i, j: (i, j)),
        out_shape=jax.ShapeDtypeStruct((m, n), out_dtype),
        compiler_params=_params(("parallel", "parallel")),
        name="proj_matmul",
    )(x, w)


def _mm_ln_kernel(a_ref, w_ref, x_ref, g_ref, b_ref, h_ref, hb_ref, *, alpha):
    z = alpha * x_ref[...] + _dot(a_ref[...], w_ref[...])
    h = _layer_norm(z, g_ref[...], b_ref[...])
    h_ref[...] = h
    hb_ref[...] = h.astype(BF16)


def _matmul_residual_ln(a, w, x, g, b, alpha):
    m, k = a.shape
    n = w.shape[1]
    tm = _pick(m, (256, 128))
    row = lambda i: (i, 0)
    fixed = lambda i: (0, 0)
    return pl.pallas_call(
        functools.partial(_mm_ln_kernel, alpha=alpha),
        grid=(m // tm,),
        in_specs=[pl.BlockSpec((tm, k), row), pl.BlockSpec((k, n), fixed),
                  pl.BlockSpec((tm, n), row), pl.BlockSpec((1, n), fixed), pl.BlockSpec((1, n), fixed)],
        out_specs=[pl.BlockSpec((tm, n), row), pl.BlockSpec((tm, n), row)],
        out_shape=[jax.ShapeDtypeStruct((m, n), F32), jax.ShapeDtypeStruct((m, n), BF16)],
        compiler_params=_params(("parallel",)),
        name="out_proj_ln",
    )(a, w, x, g.reshape(1, n), b.reshape(1, n))


def _ple_kernel(hb_ref, wg_ref, pb_ref, wp_ref, e_ref):
    gate = _sigmoid(_dot(hb_ref[...], wg_ref[...]))
    e_ref[...] = gate * _dot(pb_ref[...], wp_ref[...])


def _per_layer_embed(hb, wg, pb, wp):
    m, k = hb.shape
    n = wg.shape[1]
    kp = pb.shape[1]
    tm = _pick(m, (1024, 512, 256, 128))
    tn = _pick(n, (1024, 512, 256, 128))
    return pl.pallas_call(
        _ple_kernel,
        grid=(m // tm, n // tn),
        in_specs=[pl.BlockSpec((tm, k), lambda i, j: (i, 0)), pl.BlockSpec((k, tn), lambda i, j: (0, j)),
                  pl.BlockSpec((tm, kp), lambda i, j: (i, 0)), pl.BlockSpec((kp, tn), lambda i, j: (0, j))],
        out_specs=pl.BlockSpec((tm, tn), lambda i, j: (i, j)),
        out_shape=jax.ShapeDtypeStruct((m, n), F32),
        compiler_params=_params(("parallel", "parallel")),
        name="per_layer_embed",
    )(hb, wg, pb, wp)


def _swiglu_tile(hb, wg, wu, wd):
    g = _dot(hb, wg)
    u = _dot(hb, wu)
    a = g * _sigmoid(g) * u
    return _dot(a.astype(BF16), wd)


def _ffn_kernel(hb_ref, wg_ref, wu_ref, wd_ref, f_ref):
    j = pl.program_id(1)
    contrib = _swiglu_tile(hb_ref[...], wg_ref[...], wu_ref[...], wd_ref[...])

    @pl.when(j == 0)
    def _():
        f_ref[...] = contrib

    @pl.when(j > 0)
    def _():
        f_ref[...] += contrib


def _ffn(hb, wg, wu, wd):
    m, d = hb.shape
    f = wg.shape[1]
    tm = _pick(m, (512, 256, 128))
    tf = _pick(f, (512, 256, 128))
    return pl.pallas_call(
        _ffn_kernel,
        grid=(m // tm, f // tf),
        in_specs=[pl.BlockSpec((tm, d), lambda i, j: (i, 0)),
                  pl.BlockSpec((d, tf), lambda i, j: (0, j)),
                  pl.BlockSpec((d, tf), lambda i, j: (0, j)),
                  pl.BlockSpec((tf, d), lambda i, j: (j, 0))],
        out_specs=pl.BlockSpec((tm, d), lambda i, j: (i, 0)),
        out_shape=jax.ShapeDtypeStruct((m, d), F32),
        compiler_params=_params(("parallel", "arbitrary")),
        name="dense_ffn",
    )(hb, wg, wu, wd)


def _add_ln_kernel(h_ref, f_ref, e_ref, g_ref, b_ref, y_ref, yb_ref, *, alpha):
    z = alpha * h_ref[...] + f_ref[...] + e_ref[...]
    y = _layer_norm(z, g_ref[...], b_ref[...])
    y_ref[...] = y
    yb_ref[...] = y.astype(BF16)


def _add_ln(h, f, e, g, b, alpha):
    m, n = h.shape
    tm = _pick(m, (256, 128))
    row = lambda i: (i, 0)
    fixed = lambda i: (0, 0)
    return pl.pallas_call(
        functools.partial(_add_ln_kernel, alpha=alpha),
        grid=(m // tm,),
        in_specs=[pl.BlockSpec((tm, n), row)] * 3 + [pl.BlockSpec((1, n), fixed)] * 2,
        out_specs=[pl.BlockSpec((tm, n), row), pl.BlockSpec((tm, n), row)],
        out_shape=[jax.ShapeDtypeStruct((m, n), F32), jax.ShapeDtypeStruct((m, n), BF16)],
        compiler_params=_params(("parallel",)),
        name="ffn_residual_ln",
    )(h, f, e, g.reshape(1, n), b.reshape(1, n))


def _rwkv_prep_kernel(u_ref, p_ref, mu_ref, w0_ref, a0_ref, kk_ref, ka_ref, rk_ref,
                      w2_ref, a2_ref, g2_ref, seg_ref,
                      r_ref, w_ref, k_ref, v_ref, a_ref, b_ref, g_ref, bonus_ref, *, wa, lora):
    u = u_ref[...]
    m = u + (p_ref[...] - u) * mu_ref[...]
    r = m[:, 0:wa]
    k = m[:, wa:2 * wa]
    v = m[:, 2 * wa:3 * wa]
    xwa = m[:, 3 * wa:3 * wa + lora]
    xg = m[:, 3 * wa + lora:]
    seg = seg_ref[...]
    lw = w0_ref[...] + _dot(jnp.tanh(xwa).astype(BF16), w2_ref[...])
    w_log = -(jnp.maximum(-lw, 0.0) + jnp.log1p(jnp.exp(-jnp.abs(lw)))) - 0.5
    decay = jnp.exp(-jnp.exp(w_log))
    a = _sigmoid(a0_ref[...] + _dot(xwa.astype(BF16), a2_ref[...]))
    g = _dot(_sigmoid(xg).astype(BF16), g2_ref[...])
    kk = k * kk_ref[...]
    ss = _dot_exact_rhs01(kk * kk, seg)
    kk = kk * lax.rsqrt(jnp.maximum(ss, 1e-24))
    k2 = k * (1.0 + (a - 1.0) * ka_ref[...])
    r_ref[...] = r
    w_ref[...] = decay
    k_ref[...] = k2
    v_ref[...] = v
    a_ref[...] = -kk
    b_ref[...] = kk * a
    g_ref[...] = g
    bonus_ref[...] = _dot_exact_rhs01(r * k2 * rk_ref[...], seg) * v


def _rwkv_prep(u, prev, mu, w0, a0, k_k, k_a, r_k, w2p, a2p, g2, seg):
    n, pa = u.shape
    wa = w0.shape[-1]
    lora = w2p.shape[0]
    lg = g2.shape[0]
    tm = _pick(n, (256, 128))
    row = lambda i: (i, 0)
    fixed = lambda i: (0, 0)
    vec = lambda d: pl.BlockSpec((1, d), fixed)
    outs = [jax.ShapeDtypeStruct((n, wa), F32)] * 8
    return pl.pallas_call(
        functools.partial(_rwkv_prep_kernel, wa=wa, lora=lora),
        grid=(n // tm,),
        in_specs=[pl.BlockSpec((tm, pa), row), pl.BlockSpec((tm, pa), row), vec(pa),
                  vec(wa), vec(wa), vec(wa), vec(wa), vec(wa),
                  pl.BlockSpec((lora, wa), fixed), pl.BlockSpec((lora, wa), fixed),
                  pl.BlockSpec((lg, wa), fixed), pl.BlockSpec((wa, wa), fixed)],
        out_specs=[pl.BlockSpec((tm, wa), row)] * 8,
        out_shape=outs,
        compiler_params=_params(("parallel",)),
        name="rwkv_prep",
    )(u, prev, mu.reshape(1, pa), w0.reshape(1, wa), a0.reshape(1, wa), k_k.reshape(1, wa),
      k_a.reshape(1, wa), r_k.reshape(1, wa), w2p, a2p, g2, seg)


def _seg_sum2(x, seg):
    rows = x.shape[0]
    hi = x.astype(BF16)
    lo = (x - hi.astype(F32)).astype(BF16)
    r = _dot(jnp.concatenate([hi, lo], axis=0), seg)
    return r[0:rows] + r[rows:2 * rows]


def _rwkv_scan_kernel(r_ref, w_ref, k_ref, v_ref, a_ref, b_ref, s0_ref, seg_ref, y_ref, s_ref, yt_ref,
                      *, bg, tc, dh, npairs):
    c = pl.program_id(1)

    @pl.when(c == 0)
    def _():
        s_ref[...] = s0_ref[...]

    wa = v_ref.shape[-1]
    seg = seg_ref[...]
    lane = lax.broadcasted_iota(I32, (1, LANES), 1)
    first = lane < dh
    pad = jnp.zeros((LANES - tc, wa), F32)
    vts = [jnp.concatenate([v_ref[b], pad], axis=0).T for b in range(bg)]
    yt_ref[...] = jnp.zeros(yt_ref.shape, F32)
    for t in range(tc):
        for b in range(bg):
            for hp in range(npairs):
                cs = slice(hp * LANES, (hp + 1) * LANES)
                lo_rows = slice(hp * LANES, hp * LANES + dh)
                hi_rows = slice(hp * LANES + dh, (hp + 1) * LANES)
                s = s_ref[b, hp]
                sa = _seg_sum2(s * a_ref[b, t:t + 1, cs], seg)
                vb = jnp.where(first, vts[b][lo_rows, t:t + 1], vts[b][hi_rows, t:t + 1])
                s = s * w_ref[b, t:t + 1, cs] + sa * b_ref[b, t:t + 1, cs] + vb * k_ref[b, t:t + 1, cs]
                s_ref[b, hp] = s
                yv = _seg_sum2(s * r_ref[b, t:t + 1, cs], seg)
                yt_ref[b, lo_rows, t:t + 1] = yv[:, t:t + 1]
                yt_ref[b, hi_rows, dh + t:dh + t + 1] = yv[:, dh + t:dh + t + 1]
    first_cols = (lax.broadcasted_iota(I32, (1, wa), 1) % LANES) < dh
    for b in range(bg):
        yt = yt_ref[b].T
        y_ref[b] = jnp.where(first_cols, yt[0:tc, :], yt[dh:dh + tc, :])


def _rwkv_scan(seqs, s0, seg_pair, tc, dh):
    nb, t_len, wa = seqs[0].shape
    npairs = wa // LANES
    bg = _pick(nb, (8, 4, 2, 1))
    nchunks = t_len // tc
    seq_spec = pl.BlockSpec((bg, tc, wa), lambda g, c: (g, c, 0))
    st_spec = pl.BlockSpec((bg, npairs, dh, LANES), lambda g, c: (g, 0, 0, 0))
    return pl.pallas_call(
        functools.partial(_rwkv_scan_kernel, bg=bg, tc=tc, dh=dh, npairs=npairs),
        grid=(nb // bg, nchunks),
        in_specs=[seq_spec] * 6 + [st_spec, pl.BlockSpec((LANES, LANES), lambda g, c: (0, 0))],
        out_specs=[seq_spec, st_spec],
        out_shape=[jax.ShapeDtypeStruct((nb, t_len, wa), F32),
                   jax.ShapeDtypeStruct((nb, npairs, dh, LANES), F32)],
        scratch_shapes=[pltpu.VMEM((bg, wa, LANES), F32)],
        compiler_params=_params(("parallel", "arbitrary")),
        name="rwkv_scan",
    )(*seqs, s0, seg_pair)


def _rwkv_post_kernel(y_ref, bonus_ref, g_ref, gng_ref, gnb_ref, seg_ref, o_ref, *, dh):
    y = y_ref[...]
    seg = seg_ref[...]
    mu = _dot_exact_rhs01(y, seg) * (1.0 / dh)
    d = y - mu
    var = _dot_exact_rhs01(d * d, seg) * (1.0 / dh)
    yn = d * lax.rsqrt(var + GN_EPS_A) * gng_ref[...] + gnb_ref[...]
    o_ref[...] = ((yn + bonus_ref[...]) * g_ref[...]).astype(o_ref.dtype)


def _rwkv_post(y, bonus, g, gn_g, gn_b, seg, dh):
    n, wa = y.shape
    tm = _pick(n, (256, 128))
    row = lambda i: (i, 0)
    fixed = lambda i: (0, 0)
    return pl.pallas_call(
        functools.partial(_rwkv_post_kernel, dh=dh),
        grid=(n // tm,),
        in_specs=[pl.BlockSpec((tm, wa), row)] * 3 + [pl.BlockSpec((1, wa), fixed)] * 2
                 + [pl.BlockSpec((wa, wa), fixed)],
        out_specs=pl.BlockSpec((tm, wa), row),
        out_shape=jax.ShapeDtypeStruct((n, wa), BF16),
        compiler_params=_params(("parallel",)),
        name="rwkv_post",
    )(y, bonus, g, gn_g.reshape(1, wa), gn_b.reshape(1, wa), seg)


def _state_to_pairs(s):
    nb, h, dh, _ = s.shape
    return s.reshape(nb, h // 2, 2, dh, dh).transpose(0, 1, 3, 2, 4).reshape(nb, h // 2, dh, 2 * dh)


def _state_from_pairs(s, h, dh):
    nb = s.shape[0]
    return s.reshape(nb, h // 2, dh, 2, dh).transpose(0, 1, 3, 2, 4).reshape(nb, h, dh, dh)


def _rwkv_mix(u, shift_prev, s0_pairs, params, seg, seg_pair, dh):
    nb, t_len, pa = u.shape
    prev = jnp.concatenate([shift_prev[:, None, :], u[:, :-1]], axis=1).reshape(nb * t_len, pa)
    outs = _rwkv_prep(u.reshape(nb * t_len, pa), prev, *params["prep"], seg)
    wa = outs[0].shape[1]
    seqs = [t.reshape(nb, t_len, wa) for t in outs[:6]]
    y, s_last = _rwkv_scan(seqs, s0_pairs, seg_pair, 8, dh)
    out = _rwkv_post(y.reshape(nb * t_len, wa), outs[7], outs[6], *params["post"], seg, dh)
    return out, s_last


def _suffix_matrix(n):
    r = lax.broadcasted_iota(I32, (n, n), 0)
    c = lax.broadcasted_iota(I32, (n, n), 1)
    return (r > c).astype(BF16)


def _sb_weights(z, mask, carry, suffix):
    t = jnp.log1p(jnp.exp(-jnp.abs(z)))
    log_keep = -(jnp.maximum(z, 0.0) + t)
    log_beta = -(jnp.maximum(-z, 0.0) + t)
    if mask is not None:
        log_keep = jnp.where(mask, log_keep, 0.0)
    between = _dot_2piece_rhs01(log_keep, suffix) + carry
    wgt = jnp.exp(log_beta + between)
    if mask is not None:
        wgt = jnp.where(mask, wgt, 0.0)
    return wgt, carry + jnp.sum(log_keep, axis=-1, keepdims=True)


def _sb_prompt_kernel(q_ref, k_ref, v_ref, o_ref, acc_ref, carry_ref, *, tq, nk, dh, scale):
    qi = pl.program_id(2)
    lane = lax.broadcasted_iota(I32, (1, LANES), 1)
    heads = [(lane // dh) == hh for hh in range(LANES // dh)]
    q = q_ref[...]
    qms = [jnp.where(hm, q, 0.0).astype(BF16) for hm in heads]
    suffix = _suffix_matrix(tq)
    rows = qi * tq + lax.broadcasted_iota(I32, (tq, tq), 0)
    acc_ref[...] = jnp.zeros(acc_ref.shape, F32)
    carry_ref[...] = jnp.zeros(carry_ref.shape, F32)
    for ch in reversed(range(nk)):

        @pl.when(ch <= qi)
        def _():
            kb = k_ref[ch * tq:(ch + 1) * tq, :].astype(BF16)
            vb = v_ref[ch * tq:(ch + 1) * tq, :]
            mask = (ch * tq + lax.broadcasted_iota(I32, (tq, tq), 1)) < rows
            acc = acc_ref[...]
            for hh, (hm, qm) in enumerate(zip(heads, qms)):
                wgt, carry = _sb_weights(_dot_nt(qm, kb) * scale, mask, carry_ref[hh][:, 0:1], suffix)
                carry_ref[hh] = jnp.broadcast_to(carry, (tq, LANES))
                acc = acc + _dot(wgt.astype(BF16), jnp.where(hm, vb, 0.0).astype(BF16))
            acc_ref[...] = acc

    o_ref[...] = acc_ref[...].astype(o_ref.dtype)


def _sb_prompt(q, k, v, nb, t_len, dh):
    w = q.shape[1]
    tq = _pick(t_len, (256, 128))
    nq = t_len // tq
    return pl.pallas_call(
        functools.partial(_sb_prompt_kernel, tq=tq, nk=nq, dh=dh, scale=dh ** -0.5),
        grid=(nb, w // LANES, nq),
        in_specs=[pl.BlockSpec((tq, LANES), lambda b, h, i: (b * nq + i, h)),
                  pl.BlockSpec((t_len, LANES), lambda b, h, i: (b, h)),
                  pl.BlockSpec((t_len, LANES), lambda b, h, i: (b, h))],
        out_specs=pl.BlockSpec((tq, LANES), lambda b, h, i: (b * nq + i, h)),
        out_shape=jax.ShapeDtypeStruct((nb * t_len, w), BF16),
        scratch_shapes=[pltpu.VMEM((tq, LANES), F32), pltpu.VMEM((LANES // dh, tq, LANES), F32)],
        compiler_params=_params(("parallel", "parallel", "parallel")),
        name="sb_attn_prompt",
    )(q, k, v)


def _pages_per_step(n_pages):
    return _pick(n_pages, (4, 2, 1))


def _paged_specs(dt, w, k_page_shape, v_page_shape, gp, row_blk0, group_of_step):
    new = pl.BlockSpec((dt, w), lambda b, j, pt: (row_blk0 + b, 0))

    def pool(g, shape):
        return pl.BlockSpec((None,) + tuple(shape),
                            lambda b, j, pt: (pt[b, group_of_step(j) * gp + g],) + (0,) * len(shape))

    return ([new, new, new] + [pool(g, k_page_shape) for g in range(gp)]
            + [pool(g, v_page_shape) for g in range(gp)])


def _keys_on_lanes(pool):
    return jnp.transpose(pool, (0, 2, 3, 1))


def _padded_new_block(kn_ref, vn_ref, dt, page):
    pad = jnp.zeros((page - dt, kn_ref.shape[-1]), F32)
    kblk = jnp.concatenate([kn_ref[...], pad], axis=0)
    vblk = jnp.concatenate([vn_ref[...], pad], axis=0)
    i = lax.broadcasted_iota(I32, (dt, page), 0)
    s = lax.broadcasted_iota(I32, (dt, page), 1)
    return kblk, vblk, i, s


def _sb_sample_kernel(pt_ref, q_ref, kn_ref, vn_ref, *rest, gp, dt, page, dh, scale):
    kp, vp = rest[:gp], rest[gp:2 * gp]
    o_ref, carry_ref = rest[2 * gp:]
    j = pl.program_id(1)
    w = q_ref.shape[-1]
    npairs = w // LANES
    hpb = LANES // dh
    lane = lax.broadcasted_iota(I32, (1, LANES), 1)
    heads = [(lane // dh) == hh for hh in range(hpb)]
    q = q_ref[...]

    def lhs(hp):
        qp = q[:, hp * LANES:(hp + 1) * LANES]
        return jnp.concatenate([jnp.where(hm, qp, 0.0) for hm in heads], axis=0).astype(BF16)

    def finish(z, mask, nk, pv):
        if mask is not None:
            mask = jnp.concatenate([mask] * (npairs * hpb), axis=0)
        wgt, carry = _sb_weights(z * scale, mask, carry_ref[:, 0:1], _suffix_matrix(nk))
        carry_ref[...] = jnp.broadcast_to(carry, carry_ref.shape)
        wb = wgt.astype(BF16)
        for hp in range(npairs):
            res = pv(hp, wb[hp * hpb * dt:(hp + 1) * hpb * dt])
            out = res[0:dt]
            for hh in range(1, hpb):
                out = jnp.where(heads[hh], res[hh * dt:(hh + 1) * dt], out)
            o_ref[:, hp * LANES:(hp + 1) * LANES] += out

    @pl.when(j == 0)
    def _():
        o_ref[...] = jnp.zeros(o_ref.shape, F32)
        carry_ref[...] = jnp.zeros(carry_ref.shape, F32)
        kblk, vblk, i, s = _padded_new_block(kn_ref, vn_ref, dt, page)
        kb = kblk.astype(BF16)
        vb = vblk.astype(BF16)
        z = [_dot_nt(lhs(hp), kb[:, hp * LANES:(hp + 1) * LANES]) for hp in range(npairs)]
        finish(jnp.concatenate(z, axis=0), s < i, page,
               lambda hp, w_: _dot(w_, vb[:, hp * LANES:(hp + 1) * LANES]))

    @pl.when(j > 0)
    def _():
        def pair_t(refs, hp):
            return jnp.concatenate(
                [jnp.concatenate([r[hp * hpb + hh] for hh in range(hpb)], axis=0) for r in refs], axis=1).astype(BF16)

        z = [_dot(lhs(hp), pair_t(kp, hp)) for hp in range(npairs)]
        finish(jnp.concatenate(z, axis=0), None, gp * page, lambda hp, w_: _dot_nt(w_, pair_t(vp, hp)))


def _sb_sample(page_table, q, k, v, kpool_t, vpool_t, row0, nb, dt):
    w = q.shape[1]
    n_pages = page_table.shape[1]
    _, n_heads, dh, page = kpool_t.shape
    gp = _pages_per_step(n_pages)
    ngroups = n_pages // gp
    kern = functools.partial(_sb_sample_kernel, gp=gp, dt=dt, page=page, dh=dh, scale=dh ** -0.5)
    blk = (n_heads, dh, page)
    return pl.pallas_call(
        kern,
        grid_spec=pltpu.PrefetchScalarGridSpec(
            num_scalar_prefetch=1,
            grid=(nb, ngroups + 1),
            in_specs=_paged_specs(dt, w, blk, blk, gp, row0 // dt, lambda j: ngroups - jnp.maximum(j, 1)),
            out_specs=pl.BlockSpec((dt, w), lambda b, j, pt: (b, 0)),
            scratch_shapes=[pltpu.VMEM((w // dh * dt, LANES), F32)]),
        out_shape=jax.ShapeDtypeStruct((nb * dt, w), F32),
        compiler_params=_params(("parallel", "arbitrary")),
        name="sb_attn_sample",
    )(page_table, q, k, v, *([kpool_t] * gp), *([vpool_t] * gp))


def _t5_bias(n, rb_ref, h):
    max_exact = NUM_BUCKETS // 2
    nf = jnp.maximum(n, 1).astype(F32)
    far = max_exact + (jnp.log(nf / max_exact) / math.log(MAX_DISTANCE / max_exact)
                       * (NUM_BUCKETS - max_exact)).astype(I32)
    bucket = jnp.where(n < max_exact, n, jnp.minimum(far, NUM_BUCKETS - 1))
    bias = jnp.full(n.shape, rb_ref[0, h], F32)
    for b in range(1, NUM_BUCKETS):
        bias = jnp.where(bucket == b, rb_ref[b, h], bias)
    return bias


def _bias_prompt_kernel(rb_ref, o_ref, *, nd, blk):
    h = pl.program_id(0)
    r = lax.broadcasted_iota(I32, (blk, blk), 0)
    c = lax.broadcasted_iota(I32, (blk, blk), 1)
    for d in range(nd):
        dist = d * blk + r - c
        o_ref[d] = jnp.where(dist >= 0, _t5_bias(jnp.maximum(dist, 0), rb_ref, h), NEG_INF)


def _bias_prompt_table(rel_bias, n_heads, nd):
    return pl.pallas_call(
        functools.partial(_bias_prompt_kernel, nd=nd, blk=LANES),
        grid=(n_heads,),
        in_specs=[pl.BlockSpec(memory_space=pltpu.SMEM)],
        out_specs=pl.BlockSpec((None, nd, LANES, LANES), lambda h: (h, 0, 0, 0)),
        out_shape=jax.ShapeDtypeStruct((n_heads, nd, LANES, LANES), F32),
        compiler_params=_params(("parallel",)),
        name="diff_bias_prompt",
    )(rel_bias)


def _bias_sample_kernel(rb_ref, o_ref, *, n_heads, dt, page, n_pages):
    p = pl.program_id(0)
    is_new = p == n_pages
    past_len = n_pages * page
    i = lax.broadcasted_iota(I32, (dt, page), 0)
    s = lax.broadcasted_iota(I32, (dt, page), 1)
    dist = past_len + i - (jnp.where(is_new, past_len, p * page) + s)
    valid = jnp.logical_and(dist >= 0, jnp.logical_or(s < dt, jnp.logical_not(is_new)))
    tiles = []
    for h in range(n_heads):
        b = jnp.where(valid, _t5_bias(jnp.maximum(dist, 0), rb_ref, h), NEG_INF)
        tiles += [b, b]
    o_ref[...] = jnp.concatenate(tiles, axis=0)


def _bias_sample_table(rel_bias, n_heads, dt, page, n_pages):
    rows = 2 * n_heads * dt
    return pl.pallas_call(
        functools.partial(_bias_sample_kernel, n_heads=n_heads, dt=dt, page=page, n_pages=n_pages),
        grid=(n_pages + 1,),
        in_specs=[pl.BlockSpec(memory_space=pltpu.SMEM)],
        out_specs=pl.BlockSpec((None, rows, page), lambda p: (p, 0, 0)),
        out_shape=jax.ShapeDtypeStruct((n_pages + 1, rows, page), F32),
        compiler_params=_params(("parallel",)),
        name="diff_bias_sample",
    )(rel_bias)


def _diff_lambda(lq1_ref, lk1_ref, lq2_ref, lk2_ref, lam_init):
    s1 = jnp.sum(lq1_ref[...] * lk1_ref[...], axis=-1, keepdims=True)
    s2 = jnp.sum(lq2_ref[...] * lk2_ref[...], axis=-1, keepdims=True)
    return jnp.exp(s1) - jnp.exp(s2) + lam_init


def _sub_ln(o, g, lam_init):
    return o * lax.rsqrt(jnp.mean(o * o, -1, keepdims=True) + SUBLN_EPS) * g * (1.0 - lam_init)


def _diff_prompt_kernel(q_ref, k_ref, v_ref, tab_ref, lq1_ref, lk1_ref, lq2_ref, lk2_ref, g_ref, o_ref,
                        *, tq, t_len, dh, scale, lam_init):
    qi = pl.program_id(2)
    lane = lax.broadcasted_iota(I32, (1, LANES), 1)
    q = q_ref[...]
    kb = k_ref[...].astype(BF16)
    vb = v_ref[...].astype(BF16)
    nr = tq // LANES
    rows = []
    for r in range(nr):
        blks = []
        for c in range(t_len // LANES):
            d = qi * nr + r - c
            blks.append(jnp.where(d >= 0, tab_ref[jnp.maximum(d, 0)], NEG_INF))
        rows.append(jnp.concatenate(blks, axis=1))
    bias = jnp.concatenate(rows, axis=0)
    outs = []
    for br in range(2):
        qm = jnp.where((lane // dh) == br, q, 0.0).astype(BF16)
        s = _dot_nt(qm, kb) * scale + bias
        p = jnp.exp(s - jnp.max(s, axis=-1, keepdims=True))
        outs.append(_dot(p.astype(BF16), vb) / jnp.sum(p, axis=-1, keepdims=True))
    lam = _diff_lambda(lq1_ref, lk1_ref, lq2_ref, lk2_ref, lam_init)
    o_ref[...] = _sub_ln(outs[0] - lam * outs[1], g_ref[...], lam_init).astype(o_ref.dtype)


def _lam_specs(dh, index_map):
    return [pl.BlockSpec((1, dh), index_map)] * 4 + [pl.BlockSpec((1, 2 * dh), index_map)]


def _diff_prompt(tab, q, k, v, lams, subln_g, nb, t_len, dh, lam_init):
    w = q.shape[1]
    assert 2 * dh == LANES
    tq = _pick(t_len, (256, 128))
    nq = t_len // tq
    nd = tab.shape[1]
    fixed = lambda b, h, i: (0, 0)
    kern = functools.partial(_diff_prompt_kernel, tq=tq, t_len=t_len, dh=dh, scale=dh ** -0.5, lam_init=lam_init)
    return pl.pallas_call(
        kern,
        grid=(nb, w // LANES, nq),
        in_specs=[pl.BlockSpec((tq, LANES), lambda b, h, i: (b * nq + i, h)),
                  pl.BlockSpec((t_len, LANES), lambda b, h, i: (b, h)),
                  pl.BlockSpec((t_len, LANES), lambda b, h, i: (b, h)),
                  pl.BlockSpec((None, nd, LANES, LANES), lambda b, h, i: (h, 0, 0, 0))] + _lam_specs(dh, fixed),
        out_specs=pl.BlockSpec((tq, LANES), lambda b, h, i: (b * nq + i, h)),
        out_shape=jax.ShapeDtypeStruct((nb * t_len, w), BF16),
        compiler_params=_params(("parallel", "parallel", "parallel")),
        name="diff_attn_prompt",
    )(q, k, v, tab, *lams, subln_g)


def _diff_sample_kernel(pt_ref, q_ref, kn_ref, vn_ref, *rest, gp, ngroups, dt, page, dh, scale, lam_init):
    kp, vp = rest[:gp], rest[gp:2 * gp]
    tab_ref, lq1_ref, lk1_ref, lq2_ref, lk2_ref, g_ref, o_ref, m_ref, l_ref, acc_ref = rest[2 * gp:]
    j = pl.program_id(1)
    n_heads = q_ref.shape[-1] // LANES
    lane = lax.broadcasted_iota(I32, (1, LANES), 1)
    q = q_ref[...]

    @pl.when(j == 0)
    def _():
        m_ref[...] = jnp.full(m_ref.shape, NEG_INF, F32)
        l_ref[...] = jnp.zeros(l_ref.shape, F32)
        acc_ref[...] = jnp.zeros(acc_ref.shape, F32)

    def lhs(h):
        qp = q[:, h * LANES:(h + 1) * LANES]
        return jnp.concatenate([jnp.where((lane // dh) == br, qp, 0.0) for br in range(2)], axis=0).astype(BF16)

    def update(s, bias, values):
        s = s * scale + bias
        m_old = m_ref[:, 0:1]
        m_new = jnp.maximum(m_old, jnp.max(s, axis=-1, keepdims=True))
        p = jnp.exp(s - m_new)
        alpha = jnp.exp(m_old - m_new)
        l_ref[...] = jnp.broadcast_to(alpha * l_ref[:, 0:1] + jnp.sum(p, axis=-1, keepdims=True), l_ref.shape)
        m_ref[...] = jnp.broadcast_to(m_new, m_ref.shape)
        pb = p.astype(BF16)
        pv = [_dot(pb[h * 2 * dt:(h + 1) * 2 * dt], values(h)) for h in range(n_heads)]
        acc_ref[...] = alpha * acc_ref[...] + jnp.concatenate(pv, axis=0)

    @pl.when(j < ngroups)
    def _():
        def keys_t(h):
            return jnp.concatenate(
                [jnp.concatenate([r[2 * h], r[2 * h + 1]], axis=0) for r in kp], axis=1).astype(BF16)

        s = [_dot(lhs(h), keys_t(h)) for h in range(n_heads)]
        bias = jnp.concatenate([tab_ref[j * gp + g] for g in range(gp)], axis=1)
        update(jnp.concatenate(s, axis=0), bias,
               lambda h: jnp.concatenate([r[:, h, :] for r in vp], axis=0).astype(BF16))

    @pl.when(j == ngroups)
    def _():
        kblk, vblk, _, _ = _padded_new_block(kn_ref, vn_ref, dt, page)
        kb = kblk.astype(BF16)
        vb = vblk.astype(BF16)
        s = [_dot_nt(lhs(h), kb[:, h * LANES:(h + 1) * LANES]) for h in range(n_heads)]
        update(jnp.concatenate(s, axis=0), tab_ref[ngroups * gp], lambda h: vb[:, h * LANES:(h + 1) * LANES])
        lam = _diff_lambda(lq1_ref, lk1_ref, lq2_ref, lk2_ref, lam_init)
        for h in range(n_heads):
            a = acc_ref[h * 2 * dt:(h + 1) * 2 * dt] / l_ref[h * 2 * dt:(h + 1) * 2 * dt, 0:1]
            o_ref[:, h * LANES:(h + 1) * LANES] = _sub_ln(a[0:dt] - lam * a[dt:2 * dt], g_ref[...], lam_init)


def _diff_sample(page_table, tab, q, k, v, kpool_t, vpool, lams, subln_g, row0, nb, dt, lam_init):
    w = q.shape[1]
    n_pages = page_table.shape[1]
    _, _, dh, page = kpool_t.shape
    assert 2 * dh == LANES
    gp = _pages_per_step(n_pages)
    ngroups = n_pages // gp
    rows = 2 * (w // LANES) * dt
    fixed = lambda b, j, pt: (0, 0)
    kern = functools.partial(_diff_sample_kernel, gp=gp, ngroups=ngroups, dt=dt, page=page, dh=dh,
                             scale=dh ** -0.5, lam_init=lam_init)
    return pl.pallas_call(
        kern,
        grid_spec=pltpu.PrefetchScalarGridSpec(
            num_scalar_prefetch=1,
            grid=(nb, ngroups + 1),
            in_specs=_paged_specs(dt, w, kpool_t.shape[1:], vpool.shape[1:], gp, row0 // dt,
                                  lambda j: jnp.minimum(j, ngroups - 1))
                     + [pl.BlockSpec((n_pages + 1, rows, page), lambda b, j, pt: (0, 0, 0))]
                     + _lam_specs(dh, fixed),
            out_specs=pl.BlockSpec((dt, w), lambda b, j, pt: (b, 0)),
            scratch_shapes=[pltpu.VMEM((rows, LANES), F32)] * 3),
        out_shape=jax.ShapeDtypeStruct((nb * dt, w), F32),
        compiler_params=_params(("parallel", "arbitrary")),
        name="diff_attn_sample",
    )(page_table, q, k, v, *([kpool_t] * gp), *([vpool] * gp), tab, *lams, subln_g)


def _router_kernel(h_ref, wr_ref, idx_ref, gate_ref, cnt_ref, carry_ref, *, n_experts):
    i = pl.program_id(0)

    @pl.when(i == 0)
    def _():
        carry_ref[...] = jnp.zeros(carry_ref.shape, F32)

    tm = h_ref.shape[0]
    hh, hm, hl = _split3(h_ref[...])
    wh, wm, wl = _split3(wr_ref[...])
    logits = (_dot(hh, wh) + (_dot(hh, wm) + _dot(hm, wh))
              + (_dot(hh, wl) + _dot(hm, wm) + _dot(hl, wh)))
    lane = lax.broadcasted_iota(I32, (tm, LANES), 1)
    lg = jnp.where(lane < n_experts, logits, -jnp.inf)
    m1 = jnp.max(lg, axis=-1, keepdims=True)
    i1 = jnp.min(jnp.where(lg == m1, lane, LANES), axis=-1, keepdims=True)
    lg2 = jnp.where(lane == i1, -jnp.inf, lg)
    m2 = jnp.max(lg2, axis=-1, keepdims=True)
    i2 = jnp.min(jnp.where(lg2 == m2, lane, LANES), axis=-1, keepdims=True)
    e2 = jnp.exp(m2 - m1)
    denom = 1.0 + e2
    g1 = 1.0 / denom
    g2 = e2 / denom

    r = lax.broadcasted_iota(I32, (tm, tm), 0)
    c = lax.broadcasted_iota(I32, (tm, tm), 1)
    before = (c < r).astype(BF16)
    ranks = []
    carry = carry_ref[...]
    for sel in (i1, i2):
        onehot = lane == sel
        inside = _dot(before, onehot.astype(BF16))
        ranks.append(jnp.sum(jnp.where(onehot, carry + inside, 0.0), axis=-1, keepdims=True))
        carry = carry + jnp.sum(onehot.astype(F32), axis=0, keepdims=True)
    carry_ref[...] = carry
    cnt_ref[...] = carry
    idx = jnp.where(lane == 0, i1, jnp.where(lane == 1, i2, 0))
    idx = jnp.where(lane == 2, ranks[0].astype(I32), jnp.where(lane == 3, ranks[1].astype(I32), idx))
    idx_ref[...] = idx
    gate_ref[...] = jnp.where(lane == 0, g1, jnp.where(lane == 1, g2, 0.0))


def _router(h, router):
    n, d = h.shape
    n_experts = router.shape[1]
    wr = jnp.zeros((d, LANES), F32).at[:, :n_experts].set(router)
    tm = _pick(n, (512, 256, 128))
    return pl.pallas_call(
        functools.partial(_router_kernel, n_experts=n_experts),
        grid=(n // tm,),
        in_specs=[pl.BlockSpec((tm, d), lambda i: (i, 0)), pl.BlockSpec((d, LANES), lambda i: (0, 0))],
        out_specs=[pl.BlockSpec((tm, LANES), lambda i: (i, 0)), pl.BlockSpec((tm, LANES), lambda i: (i, 0)),
                   pl.BlockSpec((1, LANES), lambda i: (0, 0))],
        out_shape=[jax.ShapeDtypeStruct((n, LANES), I32), jax.ShapeDtypeStruct((n, LANES), F32),
                   jax.ShapeDtypeStruct((1, LANES), F32)],
        scratch_shapes=[pltpu.VMEM((1, LANES), F32)],
        compiler_params=_params(("arbitrary",)),
        name="moe_router",
    )(h, wr)


def _row_copy(src, dst, s, d, sem):
    return pltpu.make_async_copy(src.at[pl.ds(s, 1)], dst.at[pl.ds(d, 1)], sem)


def _dispatch_kernel(pos_ref, h_ref, init_ref, o_ref, sem, *, tm):
    del init_ref

    def start(r, _):
        for k in range(TOP_K):
            _row_copy(h_ref, o_ref, r, pos_ref[0, 0, TOP_K * r + k], sem).start()
        return 0

    def wait(r, _):
        for k in range(TOP_K):
            _row_copy(h_ref, o_ref, 0, 0, sem).wait()
        return 0

    lax.fori_loop(0, tm, start, 0)
    lax.fori_loop(0, tm, wait, 0)


def _dispatch(h, pos, n_rows):
    n, d = h.shape
    tm = _pick(n, (256, 128))
    pos3 = pos.reshape(n // tm, 1, tm * TOP_K)
    return pl.pallas_call(
        functools.partial(_dispatch_kernel, tm=tm),
        grid=(n // tm,),
        in_specs=[pl.BlockSpec((1, 1, tm * TOP_K), lambda i: (i, 0, 0), memory_space=pltpu.SMEM),
                  pl.BlockSpec((tm, d), lambda i: (i, 0)),
                  pl.BlockSpec(memory_space=pl.ANY)],
        out_specs=pl.BlockSpec(memory_space=pl.ANY),
        out_shape=jax.ShapeDtypeStruct((n_rows, d), h.dtype),
        scratch_shapes=[pltpu.SemaphoreType.DMA],
        input_output_aliases={2: 0},
        compiler_params=_params(("arbitrary",)),
        name="moe_dispatch",
    )(pos3, h, jnp.zeros((n_rows, d), h.dtype))


def _moe_kernel(te_ref, nt_ref, x_ref, wg_ref, wu_ref, wd_ref, init_ref, y_ref, xb_ref):
    del init_ref
    t = pl.program_id(0)
    j = pl.program_id(1)

    @pl.when(t < nt_ref[0])
    def _():
        @pl.when(j == 0)
        def _():
            xb_ref[...] = x_ref[...].astype(BF16)

        contrib = _swiglu_tile(xb_ref[...], wg_ref[...], wu_ref[...], wd_ref[...])

        @pl.when(j == 0)
        def _():
            y_ref[...] = contrib

        @pl.when(j > 0)
        def _():
            y_ref[...] += contrib


def _moe_ffn(x_sorted, tile_expert, n_tiles, wg, wu, wd, tm):
    n_rows, d = x_sorted.shape
    f = wg.shape[2]
    tf = _pick(f, (512, 256, 128))
    nf = f // tf
    max_tiles = n_rows // tm

    def live(t, j, nt):
        ok = t < nt[0]
        return jnp.minimum(t, nt[0] - 1), jnp.where(ok, j, nf - 1)

    def x_map(t, j, te, nt):
        tt, _ = live(t, j, nt)
        return tt, 0

    def w_map(t, j, te, nt):
        tt, jj = live(t, j, nt)
        return te[tt], 0, jj

    def wd_map(t, j, te, nt):
        tt, jj = live(t, j, nt)
        return te[tt], jj, 0

    return pl.pallas_call(
        _moe_kernel,
        grid_spec=pltpu.PrefetchScalarGridSpec(
            num_scalar_prefetch=2,
            grid=(max_tiles, nf),
            in_specs=[pl.BlockSpec((tm, d), x_map),
                      pl.BlockSpec((None, d, tf), w_map),
                      pl.BlockSpec((None, d, tf), w_map),
                      pl.BlockSpec((None, tf, d), wd_map),
                      pl.BlockSpec(memory_space=pl.ANY)],
            out_specs=pl.BlockSpec((tm, d), x_map),
            scratch_shapes=[pltpu.VMEM((tm, d), BF16)]),
        out_shape=jax.ShapeDtypeStruct((n_rows, d), F32),
        input_output_aliases={6: 0},
        compiler_params=_params(("arbitrary", "arbitrary")),
        name="moe_ffn",
    )(tile_expert, n_tiles, x_sorted, wg, wu, wd, jnp.zeros((n_rows, d), F32))


def _combine_ln_kernel(pos_ref, gate_ref, h_ref, e_ref, g_ref, b_ref, ys_ref, y_ref, buf_ref, sem, *, tm, alpha):
    def start(r, _):
        for k in range(TOP_K):
            _row_copy(ys_ref, buf_ref.at[k], pos_ref[0, 0, TOP_K * r + k], r, sem).start()
        return 0

    def wait(r, _):
        for k in range(TOP_K):
            _row_copy(ys_ref, buf_ref.at[k], 0, 0, sem).wait()
        return 0

    lax.fori_loop(0, tm, start, 0)
    lax.fori_loop(0, tm, wait, 0)
    gates = gate_ref[...]
    f = gates[:, 0:1] * buf_ref[0] + gates[:, 1:2] * buf_ref[1]
    z = alpha * h_ref[...] + f + e_ref[...]
    y_ref[...] = _layer_norm(z, g_ref[...], b_ref[...])


def _combine_ln(pos, gates, h, e, g, b, y_sorted, alpha):
    n, d = h.shape
    tm = _pick(n, (256, 128))
    pos3 = pos.reshape(n // tm, 1, tm * TOP_K)
    row = lambda i: (i, 0)
    fixed = lambda i: (0, 0)
    return pl.pallas_call(
        functools.partial(_combine_ln_kernel, tm=tm, alpha=alpha),
        grid=(n // tm,),
        in_specs=[pl.BlockSpec((1, 1, tm * TOP_K), lambda i: (i, 0, 0), memory_space=pltpu.SMEM),
                  pl.BlockSpec((tm, LANES), row), pl.BlockSpec((tm, d), row), pl.BlockSpec((tm, d), row),
                  pl.BlockSpec((1, d), fixed), pl.BlockSpec((1, d), fixed),
                  pl.BlockSpec(memory_space=pl.ANY)],
        out_specs=pl.BlockSpec((tm, d), row),
        out_shape=jax.ShapeDtypeStruct((n, d), F32),
        scratch_shapes=[pltpu.VMEM((TOP_K, tm, d), F32), pltpu.SemaphoreType.DMA],
        compiler_params=_params(("arbitrary",)),
        name="moe_combine_ln",
    )(pos3, gates, h, e, g.reshape(1, d), b.reshape(1, d), y_sorted)


def _moe_layout(idx, counts, n_experts, tm):
    n = idx.shape[0]
    max_tiles = (n * TOP_K) // tm + n_experts
    cnt = counts[0, :n_experts].astype(I32)
    tiles = (cnt + tm - 1) // tm
    tile_end = jnp.cumsum(tiles)
    row_start = (tile_end - tiles) * tm
    pos = row_start[idx[:, 0:TOP_K]] + idx[:, TOP_K:2 * TOP_K]
    tile_expert = jnp.minimum(jnp.searchsorted(tile_end, jnp.arange(max_tiles, dtype=I32), side="right"),
                              n_experts - 1).astype(I32)
    return pos.astype(I32), tile_expert, tile_end[-1:].astype(I32), max_tiles * tm


def _segment_matrix(w, dh):
    i = jnp.arange(w) // dh
    return (i[:, None] == i[None, :]).astype(BF16)


def kernel(x_prompt, x_sample, cache_sb_k, cache_sb_v, cache_diff_k, cache_diff_v, state_rwkv, state_rwkv_shift, page_table, p_prompt, p_sample, w_in_even, w_out_even, rwkv_mu, rwkv_w0, rwkv_w2, rwkv_a0, rwkv_a2, rwkv_g2, rwkv_k_k, rwkv_k_a, rwkv_r_k, rwkv_gn_g, rwkv_gn_b, w_in_odd, w_out_odd, diff_lq1, diff_lk1, diff_lq2, diff_lk2, diff_subln_g, rel_bias, ffn_gate, ffn_up, ffn_down, moe_router, moe_gate, moe_up, moe_down, ple_gate, ple_proj, ln1_g, ln1_b, ln2_g, ln2_b):
    nb, t_len, d = x_prompt.shape
    db, dt, _ = x_sample.shape
    depth = ple_gate.shape[0]
    alpha = (2 * depth) ** 0.25
    n_prompt = nb * t_len
    n_sample = db * dt
    h_a, dh_a = state_rwkv.shape[2], state_rwkv.shape[3]
    w_a = h_a * dh_a
    p_a = rwkv_mu.shape[1]
    lora_w, lora_a = rwkv_w2.shape[1], rwkv_a2.shape[1]
    h_b, dh_b = cache_sb_k.shape[3], cache_sb_k.shape[4]
    w_b = h_b * dh_b
    h_c, dh_c = cache_diff_v.shape[3], cache_diff_k.shape[4]
    w_c = 2 * h_c * dh_c
    n_pool, page = cache_sb_k.shape[1], cache_sb_k.shape[2]
    n_pages = page_table.shape[1]
    n_experts = moe_router.shape[2]
    assert lora_w + lora_a == LANES and dh_a * 2 == LANES

    x = jnp.concatenate([x_prompt.reshape(n_prompt, d), x_sample.reshape(n_sample, d)], axis=0)
    xb = x.astype(BF16)
    seg = _segment_matrix(w_a, dh_a)
    seg_pair = _segment_matrix(LANES, dh_a)

    sb_k, sb_v, dk, dv, st_p, st_s, sh_p, sh_s = [], [], [], [], [], [], [], []
    for i in range(depth):
        j = i // 2
        p_i = jnp.concatenate([p_prompt[i].reshape(n_prompt, -1), p_sample[i].reshape(n_sample, -1)],
                              axis=0).astype(BF16)
        if i % 2 == 0:
            proj = _matmul(xb, w_in_even[j].astype(BF16))
            q_b = proj[:, p_a:p_a + w_b]
            k_b = proj[:, p_a + w_b:p_a + 2 * w_b]
            v_b = proj[:, p_a + 2 * w_b:]

            u_p = proj[:n_prompt, :p_a].reshape(nb, t_len, p_a)
            u_s = proj[n_prompt:, :p_a].reshape(db, dt, p_a)
            w2p = jnp.concatenate([rwkv_w2[j], jnp.zeros((lora_a, w_a), F32)], axis=0).astype(BF16)
            a2p = jnp.concatenate([jnp.zeros((lora_w, w_a), F32), rwkv_a2[j]], axis=0).astype(BF16)
            rw = dict(prep=(rwkv_mu[j], rwkv_w0[j], rwkv_a0[j], rwkv_k_k[j], rwkv_k_a[j], rwkv_r_k[j].reshape(-1),
                            w2p, a2p, rwkv_g2[j].astype(BF16)),
                      post=(rwkv_gn_g[j], rwkv_gn_b[j]))
            oa_p, s_p = _rwkv_mix(u_p, jnp.zeros((nb, p_a), F32), jnp.zeros((nb, h_a // 2, dh_a, LANES), F32),
                                  rw, seg, seg_pair, dh_a)
            oa_s, s_s = _rwkv_mix(u_s, state_rwkv_shift[j], _state_to_pairs(state_rwkv[j]), rw, seg, seg_pair, dh_a)
            st_p.append(_state_from_pairs(s_p, h_a, dh_a))
            st_s.append(_state_from_pairs(s_s, h_a, dh_a))
            sh_p.append(u_p[:, -1])
            sh_s.append(u_s[:, -1])

            ob_p = _sb_prompt(q_b, k_b, v_b, nb, t_len, dh_b)
            ob_s = _sb_sample(page_table, q_b, k_b, v_b, _keys_on_lanes(cache_sb_k[j]), _keys_on_lanes(cache_sb_v[j]),
                              n_prompt, db, dt)
            sb_k.append(k_b)
            sb_v.append(v_b)

            mix_in = jnp.concatenate([jnp.concatenate([oa_p, oa_s], axis=0),
                                      jnp.concatenate([ob_p, ob_s.astype(BF16)], axis=0)], axis=1)
            h, hb = _matmul_residual_ln(mix_in, w_out_even[j].astype(BF16), x, ln1_g[i], ln1_b[i], alpha)
            f = _ffn(hb, ffn_gate[j].astype(BF16), ffn_up[j].astype(BF16), ffn_down[j].astype(BF16))
            e = _per_layer_embed(hb, ple_gate[i].astype(BF16), p_i, ple_proj[i].astype(BF16))
            x, xb = _add_ln(h, f, e, ln2_g[i], ln2_b[i], alpha)
        else:
            proj = _matmul(xb, w_in_odd[j].astype(BF16))
            hd = h_c * dh_c
            q = proj[:, :2 * hd]
            k = proj[:, 2 * hd:4 * hd]
            v = proj[:, 4 * hd:]
            lam_init = 0.8 - 0.6 * math.exp(-0.3 * i)
            lams = [t[j].reshape(1, dh_c) for t in (diff_lq1, diff_lk1, diff_lq2, diff_lk2)]
            g_sub = diff_subln_g[j].reshape(1, 2 * dh_c)
            o_p = _diff_prompt(_bias_prompt_table(rel_bias, h_c, t_len // LANES), q, k, v, lams, g_sub,
                               nb, t_len, dh_c, lam_init)
            o_s = _diff_sample(page_table, _bias_sample_table(rel_bias, h_c, dt, page, n_pages), q, k, v,
                               _keys_on_lanes(cache_diff_k[j]), cache_diff_v[j], lams, g_sub, n_prompt, db, dt, lam_init)
            o = jnp.concatenate([o_p, o_s.astype(BF16)], axis=0)
            dk.append(k)
            dv.append(v)

            h, hb = _matmul_residual_ln(o, w_out_odd[j].astype(BF16), x, ln1_g[i], ln1_b[i], alpha)
            idx, gates, counts = _router(h, moe_router[j])
            tm_moe = 512 if (h.shape[0] * TOP_K) % 512 == 0 else 128
            pos, tile_expert, n_tiles, n_rows = _moe_layout(idx, counts, n_experts, tm_moe)
            x_sorted = _dispatch(h, pos.reshape(-1), n_rows)
            y_sorted = _moe_ffn(x_sorted, tile_expert, n_tiles, moe_gate[j].astype(BF16), moe_up[j].astype(BF16),
                                moe_down[j].astype(BF16), tm_moe)
            e = _per_layer_embed(hb, ple_gate[i].astype(BF16), p_i, ple_proj[i].astype(BF16))
            x = _combine_ln(pos.reshape(-1), gates, h, e, ln2_g[i], ln2_b[i], y_sorted, alpha)
            xb = x.astype(BF16)

    def split(t, shape_p, shape_s):
        return t[:n_prompt].reshape(shape_p), t[n_prompt:].reshape(shape_s)

    y_p, y_s = split(x, (nb, t_len, d), (db, dt, d))
    kb = [split(t, (nb, t_len, h_b, dh_b), (db, dt, h_b, dh_b)) for t in sb_k]
    vb = [split(t, (nb, t_len, h_b, dh_b), (db, dt, h_b, dh_b)) for t in sb_v]
    kc = [split(t, (nb, t_len, 2 * h_c, dh_c), (db, dt, 2 * h_c, dh_c)) for t in dk]
    vc = [split(t, (nb, t_len, h_c, 2 * dh_c), (db, dt, h_c, 2 * dh_c)) for t in dv]
    stack = lambda parts, which: jnp.stack([p[which] for p in parts])
    return (y_p, y_s,
            stack(kb, 0), stack(vb, 0), stack(kc, 0), stack(vc, 0), jnp.stack(st_p), jnp.stack(sh_p),
            stack(kb, 1), stack(vb, 1), stack(kc, 1), stack(vc, 1), jnp.stack(st_s), jnp.stack(sh_s))
```
